```python
import jax, jax.numpy as jnp
from jax import lax
import numpy as np

D_MODEL = 2048
BATCH = 8
SEQ = 4096
DEPTH = 1

N_HEADS_ATTN = 8
HEAD_DIM = 128
D_ATTN = N_HEADS_ATTN * HEAD_DIM
MOBA_BLOCK = 256
MOBA_TOPK = 3
Q_CHUNK = 16
ROPE_THETA = 500000.0
ROPE_DIM = HEAD_DIM // 4
N_GROUPS_SGU = 8
SGU_GROUP_DIM = 128
D_SGU = N_GROUPS_SGU * SGU_GROUP_DIM
SGU_CHUNK = 128
N_EXPERT_GROUPS = 4
EXPERTS_PER_GROUP = 8
N_EXPERTS = N_EXPERT_GROUPS * EXPERTS_PER_GROUP
D_FF_EXPERT = 1024
TOP_K_INNER = 2
DISPATCH_BLOCK = 256
EPS = 1e-6
D_IN_PROJ = 3 * D_ATTN + 2 * D_SGU + 2 * D_MODEL
SPLIT_POINTS = (D_ATTN, 2 * D_ATTN, 3 * D_ATTN, 3 * D_ATTN + D_SGU,
                3 * D_ATTN + 2 * D_SGU, 3 * D_ATTN + 2 * D_SGU + D_MODEL)

kernel_name = "hybrid_moba_sgu_hiermoe_block"


def rms_norm(x, g):
    xf = x.astype(jnp.float32)
    y = xf * lax.rsqrt(jnp.mean(xf * xf, axis=-1, keepdims=True) + EPS)
    return (y * g.astype(jnp.float32)).astype(x.dtype)


def layer_norm(x, g, b):
    xf = x.astype(jnp.float32)
    mu = jnp.mean(xf, axis=-1, keepdims=True)
    var = jnp.mean(jnp.square(xf - mu), axis=-1, keepdims=True)
    y = (xf - mu) * lax.rsqrt(var + EPS)
    return (y * g.astype(jnp.float32) + b.astype(jnp.float32)).astype(x.dtype)


def partial_rope(x, pos):
    half = ROPE_DIM // 2
    inv = ROPE_THETA ** (-jnp.arange(half, dtype=jnp.float32) * 2.0 / ROPE_DIM)
    ang = pos.astype(jnp.float32)[:, None] * inv[None, :]
    cos = jnp.cos(ang).astype(x.dtype)
    sin = jnp.sin(ang).astype(x.dtype)
    x1 = x[..., :half]
    x2 = x[..., half:ROPE_DIM]
    return jnp.concatenate([x1 * cos - x2 * sin, x2 * cos + x1 * sin, x[..., ROPE_DIM:]], axis=-1)


def moba_attention(q, k, v):
    B, S, H, dh = q.shape
    pos = jnp.arange(S)
    q = partial_rope(q.transpose(0, 2, 1, 3), pos)
    k = partial_rope(k.transpose(0, 2, 1, 3), pos)
    v = v.transpose(0, 2, 1, 3)
    n_blk = -(-S // MOBA_BLOCK)
    s_pad = n_blk * MOBA_BLOCK
    pad = ((0, 0), (0, 0), (0, s_pad - S), (0, 0))
    kb = jnp.pad(k, pad).reshape(B, H, n_blk, MOBA_BLOCK, dh)
    vb = jnp.pad(v, pad).reshape(B, H, n_blk, MOBA_BLOCK, dh)
    k_mean = jnp.mean(kb.astype(jnp.float32), axis=3).astype(k.dtype)
    k_eff = min(MOBA_TOPK, n_blk)
    n_sel = k_eff * MOBA_BLOCK
    scale = HEAD_DIM ** -0.5
    n_qc = S // Q_CHUNK
    q_chunks = q.reshape(B, H, n_qc, Q_CHUNK, dh).transpose(2, 0, 1, 3, 4)
    b_idx = jnp.arange(B)[:, None, None, None]
    h_idx = jnp.arange(H)[None, :, None, None]
    blk_ids = jnp.arange(n_blk)

    def chunk(args):
        qc, c = args
        q0 = c * Q_CHUNK
        blk = q0 // MOBA_BLOCK
        qpos = q0 + jnp.arange(Q_CHUNK)
        gate = jnp.einsum('bhqd,bhnd->bhqn', qc, k_mean).astype(jnp.float32)
        gate = jnp.where(blk_ids < blk, gate, -jnp.inf)
        _, sel = lax.top_k(gate, k_eff)
        valid = sel < blk
        k_sel = kb[b_idx, h_idx, sel]
        v_sel = vb[b_idx, h_idx, sel]
        s_sel = jnp.einsum('bhqd,bhqjkd->bhqjk', qc, k_sel).astype(jnp.float32) * scale
        s_sel = jnp.where(valid[..., None], s_sel, -jnp.inf).reshape(B, H, Q_CHUNK, n_sel)
        k_own = lax.dynamic_index_in_dim(kb, blk, axis=2, keepdims=False)
        v_own = lax.dynamic_index_in_dim(vb, blk, axis=2, keepdims=False)
        s_own = jnp.einsum('bhqd,bhkd->bhqk', qc, k_own).astype(jnp.float32) * scale
        kpos = blk * MOBA_BLOCK + jnp.arange(MOBA_BLOCK)
        s_own = jnp.where(kpos[None, :] <= qpos[:, None], s_own, -jnp.inf)
        p = jax.nn.softmax(jnp.concatenate([s_sel, s_own], axis=-1), axis=-1).astype(v.dtype)
        p_sel = p[..., :n_sel].reshape(B, H, Q_CHUNK, k_eff, MOBA_BLOCK)
        o = (jnp.einsum('bhqjk,bhqjkd->bhqd', p_sel, v_sel)
             + jnp.einsum('bhqk,bhkd->bhqd', p[..., n_sel:], v_own))
        return o

    out = lax.map(chunk, (q_chunks, jnp.arange(n_qc)))
    return out.transpose(1, 0, 3, 2, 4).reshape(B, S, H * dh)


def spatial_gating(u, vg, sgu_w, sgu_b, ln_g, ln_b):
    B, S, _ = u.shape
    u = jax.nn.gelu(u)
    vg = layer_norm(jax.nn.gelu(vg), ln_g, ln_b)
    n_ch = S // SGU_CHUNK
    vr = vg.reshape(B, n_ch, SGU_CHUNK, N_GROUPS_SGU, SGU_GROUP_DIM)
    causal = jnp.tril(jnp.ones((SGU_CHUNK, SGU_CHUNK), dtype=bool))
    w = jnp.where(causal[None], sgu_w, jnp.zeros_like(sgu_w))
    mixed = jnp.einsum('gts,bnsgc->bntgc', w, vr) + sgu_b.T[None, None, :, :, None]
    return u * mixed.reshape(B, S, D_SGU)


def hier_moe(h, w_rg, b_rg, w_re, b_re, w_gate, w_up, w_down):
    B, S, D = h.shape
    T = B * S
    xt = h.reshape(T, D)
    g_logits = (xt @ w_rg + b_rg).astype(jnp.float32)
    grp = jnp.argmax(g_logits, axis=-1)
    p_grp = jnp.take_along_axis(jax.nn.softmax(g_logits, axis=-1), grp[:, None], axis=-1)
    e_logits = (xt @ w_re + b_re).astype(jnp.float32).reshape(T, N_EXPERT_GROUPS, EXPERTS_PER_GROUP)
    e_logits = jnp.take_along_axis(e_logits, grp[:, None, None], axis=1)[:, 0]
    top_w, top_i = lax.top_k(jax.nn.softmax(e_logits, axis=-1), TOP_K_INNER)
    top_w = top_w / jnp.sum(top_w, axis=-1, keepdims=True)
    weights = p_grp * top_w
    experts = grp[:, None].astype(jnp.int32) * EXPERTS_PER_GROUP + top_i.astype(jnp.int32)

    n_assign = T * TOP_K_INNER
    flat_e = experts.reshape(n_assign)
    flat_w = weights.reshape(n_assign)
    flat_tok = jnp.repeat(jnp.arange(T, dtype=jnp.int32), TOP_K_INNER)
    order = jnp.argsort(flat_e)
    se = flat_e[order]
    counts = jax.ops.segment_sum(jnp.ones_like(flat_e), flat_e, num_segments=N_EXPERTS)
    padded = (counts + DISPATCH_BLOCK - 1) // DISPATCH_BLOCK * DISPATCH_BLOCK
    pad_end = jnp.cumsum(padded)
    pad_start = pad_end - padded
    start = jnp.cumsum(counts) - counts
    dest = pad_start[se] + jnp.arange(n_assign, dtype=jnp.int32) - start[se]
    n_rows = -(-n_assign // DISPATCH_BLOCK) * DISPATCH_BLOCK + N_EXPERTS * DISPATCH_BLOCK
    n_blocks = n_rows // DISPATCH_BLOCK
    row_tok = jnp.full((n_rows,), T, dtype=jnp.int32).at[dest].set(flat_tok[order])
    row_w = jnp.zeros((n_rows,), jnp.float32).at[dest].set(flat_w[order])
    blk_e = jnp.minimum(jnp.searchsorted(pad_end, jnp.arange(n_blocks) * DISPATCH_BLOCK, side='right'),
                        N_EXPERTS - 1).astype(jnp.int32)
    x_pad = jnp.concatenate([xt, jnp.zeros((1, D), xt.dtype)], axis=0)

    def expert_block(args):
        tok, e = args
        xb = x_pad[tok]
        hid = jax.nn.silu(xb @ w_gate[e]) * (xb @ w_up[e])
        return hid @ w_down[e]

    ys = lax.map(expert_block, (row_tok.reshape(n_blocks, DISPATCH_BLOCK), blk_e))
    ys = ys.reshape(n_rows, D) * row_w[:, None].astype(xt.dtype)
    out = jnp.zeros((T + 1, D), xt.dtype).at[row_tok].add(ys)
    return out[:T].reshape(B, S, D)


def hybrid_layer(x, norm_mix_g, w_in, sgu_w, sgu_b, sgu_ln_g, sgu_ln_b, w_proj_attn, w_proj_sgu,
                 w_out, norm_ffn_g, w_router_group, b_router_group, w_router_expert, b_router_expert,
                 w_exp_gate, w_exp_up, w_exp_down):
    B, S, _ = x.shape
    h = rms_norm(x, norm_mix_g)
    proj = h @ w_in
    q, k, v, u, vg, gate_attn, gate_sgu = jnp.split(proj, SPLIT_POINTS, axis=-1)
    shp = (B, S, N_HEADS_ATTN, HEAD_DIM)
    attn = moba_attention(q.reshape(shp), k.reshape(shp), v.reshape(shp))
    sgu = spatial_gating(u, vg, sgu_w, sgu_b, sgu_ln_g, sgu_ln_b)
    merged = (jax.nn.sigmoid(gate_attn) * (attn @ w_proj_attn)
              + jax.nn.sigmoid(gate_sgu) * (sgu @ w_proj_sgu))
    x = x + merged @ w_out
    x = x + hier_moe(rms_norm(x, norm_ffn_g), w_router_group, b_router_group, w_router_expert,
                     b_router_expert, w_exp_gate, w_exp_up, w_exp_down)
    return x


def setup_inputs(seed: int = 0) -> dict:
    key = jax.random.key(seed)
    ks = jax.random.split(key, 20)
    L = DEPTH

    def nrm(k, shape, scale):
        return jax.random.normal(k, shape, jnp.float32) * scale

    return {
        "x": nrm(ks[0], (BATCH, SEQ, D_MODEL), 1.0),
        "norm_mix_g": 1.0 + nrm(ks[1], (L, D_MODEL), 0.01),
        "w_in": nrm(ks[2], (L, D_MODEL, D_IN_PROJ), D_MODEL ** -0.5),
        "sgu_w": nrm(ks[3], (L, N_GROUPS_SGU, SGU_CHUNK, SGU_CHUNK), SGU_CHUNK ** -0.5),
        "sgu_b": 1.0 + nrm(ks[4], (L, N_GROUPS_SGU, SGU_CHUNK), 0.01),
        "sgu_ln_g": 1.0 + nrm(ks[5], (L, D_SGU), 0.01),
        "sgu_ln_b": nrm(ks[6], (L, D_SGU), 0.01),
        "w_proj_attn": nrm(ks[7], (L, D_ATTN, D_MODEL), D_ATTN ** -0.5),
        "w_proj_sgu": nrm(ks[8], (L, D_SGU, D_MODEL), D_SGU ** -0.5),
        "w_out": nrm(ks[9], (L, D_MODEL, D_MODEL), D_MODEL ** -0.5),
        "norm_ffn_g": 1.0 + nrm(ks[10], (L, D_MODEL), 0.01),
        "w_router_group": nrm(ks[11], (L, D_MODEL, N_EXPERT_GROUPS), D_MODEL ** -0.5),
        "b_router_group": nrm(ks[12], (L, N_EXPERT_GROUPS), 0.01),
        "w_router_expert": nrm(ks[13], (L, D_MODEL, N_EXPERTS), D_MODEL ** -0.5),
        "b_router_expert": nrm(ks[14], (L, N_EXPERTS), 0.01),
        "w_exp_gate": nrm(ks[15], (L, N_EXPERTS, D_MODEL, D_FF_EXPERT), D_MODEL ** -0.5),
        "w_exp_up": nrm(ks[16], (L, N_EXPERTS, D_MODEL, D_FF_EXPERT), D_MODEL ** -0.5),
        "w_exp_down": nrm(ks[17], (L, N_EXPERTS, D_FF_EXPERT, D_MODEL), D_FF_EXPERT ** -0.5),
        "norm_final_g": 1.0 + nrm(ks[18], (D_MODEL,), 0.01),
    }


def reference(x, norm_mix_g, w_in, sgu_w, sgu_b, sgu_ln_g, sgu_ln_b, w_proj_attn, w_proj_sgu,
              w_out, norm_ffn_g, w_router_group, b_router_group, w_router_expert, b_router_expert,
              w_exp_gate, w_exp_up, w_exp_down, norm_final_g):
    for l in range(DEPTH):
        x = hybrid_layer(x, norm_mix_g[l], w_in[l], sgu_w[l], sgu_b[l], sgu_ln_g[l], sgu_ln_b[l],
                         w_proj_attn[l], w_proj_sgu[l], w_out[l], norm_ffn_g[l],
                         w_router_group[l], b_router_group[l], w_router_expert[l], b_router_expert[l],
                         w_exp_gate[l], w_exp_up[l], w_exp_down[l])
    return rms_norm(x, norm_final_g)
```

```python
import functools

import jax
import jax.numpy as jnp
from jax import lax
from jax.experimental import pallas as pl
from jax.experimental.pallas import tpu as pltpu

F32 = jnp.float32
BF16 = jnp.bfloat16

N_HEADS = 8
HEAD_DIM = 128
D_ATTN = N_HEADS * HEAD_DIM
MOBA_BLOCK = 256
MOBA_TOPK = 3
ROPE_THETA = 500000.0
ROPE_DIM = HEAD_DIM // 4
ROPE_HALF = ROPE_DIM // 2
N_GROUPS_SGU = 8
SGU_GROUP_DIM = 128
D_SGU = N_GROUPS_SGU * SGU_GROUP_DIM
SGU_CHUNK = 128
N_EXPERT_GROUPS = 4
EXPERTS_PER_GROUP = 8
N_EXPERTS = N_EXPERT_GROUPS * EXPERTS_PER_GROUP
TOP_K_INNER = 2
DISPATCH_BLOCK = 256
EPS = 1e-6

LANES = 128
VMEM_LIMIT_BYTES = 56 * 1024 * 1024
MASK_BIAS = -1e30

INPROJ_TM = 1024
INPROJ_TN = 1024
MERGE_TM = 256
ROW_TILE = 256


def _dot(a, b):
    return jnp.dot(a, b, preferred_element_type=F32)


def _dot_nt(a, b):
    return lax.dot_general(a, b, (((1,), (1,)), ((), ())), preferred_element_type=F32)


def _sigmoid(x):
    return 1.0 / (1.0 + jnp.exp(-x))


def _gelu_tanh(x):
    return 0.5 * x * (1.0 + jnp.tanh(0.7978845608028654 * (x + 0.044715 * (x * x * x))))


def _inproj_kernel(x_ref, g_ref, w_ref, o_ref, h_ref):
    @pl.when(pl.program_id(1) == 0)
    def _():
        x = x_ref[...]
        ms = jnp.mean(x * x, axis=-1, keepdims=True)
        h_ref[...] = (x * lax.rsqrt(ms + EPS) * g_ref[...]).astype(BF16)

    o_ref[...] = _dot(h_ref[...], w_ref[...]).astype(o_ref.dtype)


def _inproj(x2, g, w_bf16):
    t, d = x2.shape
    n = w_bf16.shape[1]
    tm, tn = min(INPROJ_TM, t), INPROJ_TN
    return pl.pallas_call(
        _inproj_kernel,
        grid=(t // tm, n // tn),
        in_specs=[
            pl.BlockSpec((tm, d), lambda i, j: (i, 0)),
            pl.BlockSpec((1, d), lambda i, j: (0, 0)),
            pl.BlockSpec((d, tn), lambda i, j: (0, j)),
        ],
        out_specs=pl.BlockSpec((tm, tn), lambda i, j: (i, j)),
        out_shape=jax.ShapeDtypeStruct((t, n), BF16),
        scratch_shapes=[pltpu.VMEM((tm, d), BF16)],
        compiler_params=pltpu.CompilerParams(
            dimension_semantics=("parallel", "arbitrary"), vmem_limit_bytes=VMEM_LIMIT_BYTES),
        name="inproj",
    )(x2, g.reshape(1, d), w_bf16)


def _attn_kernel(q_ref, k_ref, v_ref, c_ref, s1_ref, s2_ref, o_ref, kaug_ref, km_ref, *, n_blk):
    blk = MOBA_BLOCK
    scale = HEAD_DIM ** -0.5
    lane = lax.broadcasted_iota(jnp.int32, (blk, LANES), 1)

    def rope(xb, r0):
        c = c_ref[pl.ds(r0, blk), :]
        s1 = s1_ref[pl.ds(r0, blk), :]
        s2 = s2_ref[pl.ds(r0, blk), :]
        return (xb * c + pltpu.roll(xb, LANES - ROPE_HALF, 1) * s1
                + pltpu.roll(xb, ROPE_HALF, 1) * s2)

    km_ref[...] = jnp.zeros_like(km_ref)

    def kprep(j, carry):
        r0 = pl.multiple_of(j * blk, blk)
        kr = rope(k_ref[pl.ds(r0, blk), :].astype(F32), r0)
        kaug_ref[pl.ds(r0, blk), 0:LANES] = kr.astype(BF16)
        kaug_ref[pl.ds(r0, blk), LANES:2 * LANES] = jnp.where(lane == j, 1.0, 0.0).astype(BF16)
        km_ref[pl.ds(j, 1), :] = jnp.mean(kr, axis=0, keepdims=True)
        return carry

    lax.fori_loop(0, n_blk, kprep, 0)

    row_i = lax.broadcasted_iota(jnp.int32, (blk, blk), 0)
    col_i = lax.broadcasted_iota(jnp.int32, (blk, blk), 1)

    def qblock(i, carry):
        r0 = pl.multiple_of(i * blk, blk)
        qr = (rope(q_ref[pl.ds(r0, blk), :].astype(F32), r0) * scale).astype(BF16)
        km = km_ref[...]
        km_hi = km.astype(BF16)
        km_lo = (km - km_hi.astype(F32)).astype(BF16)
        gate = _dot_nt(qr, km_hi) + _dot_nt(qr, km_lo)
        g = jnp.where(lane < i, gate, -jnp.inf)
        sel = lane < 0
        for _ in range(MOBA_TOPK):
            mx = jnp.max(g, axis=-1, keepdims=True)
            idx = jnp.min(jnp.where(g == mx, lane, LANES), axis=-1, keepdims=True)
            hit = lane == idx
            sel = jnp.logical_or(sel, hit)
            g = jnp.where(hit, -jnp.inf, g)
        allowed = jnp.logical_or(jnp.logical_and(sel, lane < i), lane == i)
        bias = jnp.where(allowed, 0.0, MASK_BIAS).astype(BF16)
        qaug = jnp.concatenate([qr, bias], axis=1)

        s = _dot_nt(qaug, kaug_ref[pl.ds(r0, blk), :])
        s = jnp.where(col_i <= row_i, s, -jnp.inf)
        m = jnp.max(s, axis=-1, keepdims=True)
        p = jnp.exp(s - m)
        l = jnp.sum(p, axis=-1, keepdims=True)
        acc = _dot(p.astype(BF16), v_ref[pl.ds(r0, blk), :])

        def kv(j, mla):
            m, l, acc = mla
            c0 = pl.multiple_of(j * blk, blk)
            s = _dot_nt(qaug, kaug_ref[pl.ds(c0, blk), :])
            m_new = jnp.maximum(m, jnp.max(s, axis=-1, keepdims=True))
            a = jnp.exp(m - m_new)
            p = jnp.exp(s - m_new)
            l = a * l + jnp.sum(p, axis=-1, keepdims=True)
            acc = a * acc + _dot(p.astype(BF16), v_ref[pl.ds(c0, blk), :])
            return m_new, l, acc

        m, l, acc = lax.fori_loop(0, i, kv, (m, l, acc))
        o_ref[pl.ds(r0, blk), :] = (acc / l).astype(o_ref.dtype)
        return carry

    lax.fori_loop(0, n_blk, qblock, 0)


def _rope_tables(s):
    inv = ROPE_THETA ** (-jnp.arange(ROPE_HALF, dtype=F32) * 2.0 / ROPE_DIM)
    ang = jnp.arange(s).astype(F32)[:, None] * inv[None, :]
    cos, sin = jnp.cos(ang), jnp.sin(ang)
    zeros = jnp.zeros((s, LANES - ROPE_DIM), F32)
    zh = jnp.zeros((s, ROPE_HALF), F32)
    c = jnp.concatenate([cos, cos, zeros + 1.0], axis=1)
    s1 = jnp.concatenate([-sin, zh, zeros], axis=1)
    s2 = jnp.concatenate([zh, sin, zeros], axis=1)
    return c, s1, s2


def _attention(proj, b, s):
    assert s % MOBA_BLOCK == 0
    n_blk = s // MOBA_BLOCK
    assert n_blk <= LANES
    c, s1, s2 = _rope_tables(s)
    tab = pl.BlockSpec((s, LANES), lambda bi, h: (0, 0))
    return pl.pallas_call(
        functools.partial(_attn_kernel, n_blk=n_blk),
        grid=(b, N_HEADS),
        in_specs=[
            pl.BlockSpec((s, HEAD_DIM), lambda bi, h: (bi, h)),
            pl.BlockSpec((s, HEAD_DIM), lambda bi, h: (bi, N_HEADS + h)),
            pl.BlockSpec((s, HEAD_DIM), lambda bi, h: (bi, 2 * N_HEADS + h)),
            tab, tab, tab,
        ],
        out_specs=pl.BlockSpec((s, HEAD_DIM), lambda bi, h: (bi, h)),
        out_shape=jax.ShapeDtypeStruct((b * s, D_ATTN), BF16),
        scratch_shapes=[pltpu.VMEM((s, 2 * LANES), BF16), pltpu.VMEM((LANES, LANES), F32)],
        compiler_params=pltpu.CompilerParams(
            dimension_semantics=("parallel", "arbitrary"), vmem_limit_bytes=VMEM_LIMIT_BYTES),
        name="moba_attn",
    )(proj, proj, proj, c, s1, s2)


def _merge_kernel(x_ref, at_ref, u_ref, vg_ref, ga0_ref, ga1_ref, gs0_ref, gs1_ref,
                  sw_ref, sb_ref, lng_ref, lnb_ref, wpa_ref, wpb_ref, wo_ref, g2_ref,
                  wrh_ref, wrl_ref, br_ref,
                  x1_ref, h2_ref, rt_ref,
                  wm_ref, vgn_ref, sgu_ref, mg_ref):
    tm = x_ref.shape[0]
    d = x_ref.shape[1]
    half = d // 2

    @pl.when(pl.program_id(0) == 0)
    def _():
        r = lax.broadcasted_iota(jnp.int32, (SGU_CHUNK, SGU_CHUNK), 0)
        c = lax.broadcasted_iota(jnp.int32, (SGU_CHUNK, SGU_CHUNK), 1)
        for g in range(N_GROUPS_SGU):
            wm_ref[g] = jnp.where(c <= r, sw_ref[g], 0.0).astype(BF16)

    vg = _gelu_tanh(vg_ref[...].astype(F32))
    mu = jnp.mean(vg, axis=-1, keepdims=True)
    var = jnp.mean(jnp.square(vg - mu), axis=-1, keepdims=True)
    vgn_ref[...] = ((vg - mu) * lax.rsqrt(var + EPS) * lng_ref[...] + lnb_ref[...]).astype(BF16)
    for c in range(tm // SGU_CHUNK):
        rows = slice(c * SGU_CHUNK, (c + 1) * SGU_CHUNK)
        for g in range(N_GROUPS_SGU):
            cols = slice(g * SGU_GROUP_DIM, (g + 1) * SGU_GROUP_DIM)
            mixed = _dot(wm_ref[g], vgn_ref[rows, cols]) + sb_ref[:, cols]
            u = _gelu_tanh(u_ref[rows, cols].astype(F32))
            sgu_ref[rows, cols] = (u * mixed).astype(BF16)

    a = _dot(at_ref[...], wpa_ref[...])
    s = _dot(sgu_ref[...], wpb_ref[...])
    mg_ref[:, 0:half] = (_sigmoid(ga0_ref[...].astype(F32)) * a[:, 0:half]
                         + _sigmoid(gs0_ref[...].astype(F32)) * s[:, 0:half]).astype(BF16)
    mg_ref[:, half:d] = (_sigmoid(ga1_ref[...].astype(F32)) * a[:, half:d]
                         + _sigmoid(gs1_ref[...].astype(F32)) * s[:, half:d]).astype(BF16)
    x1 = x_ref[...] + _dot(mg_ref[...], wo_ref[...])
    x1_ref[...] = x1

    h2 = x1 * lax.rsqrt(jnp.mean(x1 * x1, axis=-1, keepdims=True) + EPS) * g2_ref[...]
    h2_ref[...] = h2
    hi = h2.astype(BF16)
    lo = (h2 - hi.astype(F32)).astype(BF16)
    logits = (_dot(hi, wrh_ref[...]) + _dot(lo, wrh_ref[...]) + _dot(hi, wrl_ref[...])) + br_ref[...]

    lane = lax.broadcasted_iota(jnp.int32, (tm, LANES), 1)
    neg = -jnp.inf
    gl = jnp.where(lane < N_EXPERT_GROUPS, logits, neg)
    gmax = jnp.max(gl, axis=-1, keepdims=True)
    grp = jnp.min(jnp.where(gl == gmax, lane, LANES), axis=-1, keepdims=True)
    p_grp = 1.0 / jnp.sum(jnp.exp(gl - gmax), axis=-1, keepdims=True)
    e0 = N_EXPERT_GROUPS + grp * EXPERTS_PER_GROUP
    emask = jnp.logical_and(lane >= e0, lane < e0 + EXPERTS_PER_GROUP)
    el = jnp.where(emask, logits, neg)
    emax = jnp.max(el, axis=-1, keepdims=True)
    ex = jnp.exp(el - emax)
    probs = jnp.where(emask, ex / jnp.sum(ex, axis=-1, keepdims=True), -1.0)
    p1 = jnp.max(probs, axis=-1, keepdims=True)
    i1 = jnp.min(jnp.where(probs == p1, lane, LANES), axis=-1, keepdims=True)
    probs2 = jnp.where(lane == i1, -1.0, probs)
    p2 = jnp.max(probs2, axis=-1, keepdims=True)
    i2 = jnp.min(jnp.where(probs2 == p2, lane, LANES), axis=-1, keepdims=True)
    den = p1 + p2
    w1 = p_grp * (p1 / den)
    w2 = p_grp * (p2 / den)
    ex1 = (i1 - N_EXPERT_GROUPS).astype(F32)
    ex2 = (i2 - N_EXPERT_GROUPS).astype(F32)
    rt_ref[...] = jnp.where(lane == 0, ex1, jnp.where(lane == 1, ex2,
                            jnp.where(lane == 2, w1, jnp.where(lane == 3, w2, 0.0))))


def _merge(x2, attn, proj, sgu_w, sgu_b, ln_g, ln_b, wpa, wpb, wo, g2, wr_hi, wr_lo, br):
    t, d = x2.shape
    tm = MERGE_TM
    assert d == 2 * D_ATTN and t % tm == 0
    u_blk = 3 * D_ATTN // D_SGU
    sb_full = jnp.repeat(sgu_b.T, SGU_GROUP_DIM, axis=1)

    def col(j):
        return pl.BlockSpec((tm, D_SGU), lambda i: (i, j))

    def const(shape):
        return pl.BlockSpec(shape, lambda i: tuple(0 for _ in shape))

    return pl.pallas_call(
        _merge_kernel,
        grid=(t // tm,),
        in_specs=[
            pl.BlockSpec((tm, d), lambda i: (i, 0)),
            pl.BlockSpec((tm, D_ATTN), lambda i: (i, 0)),
            col(u_blk), col(u_blk + 1), col(u_blk + 2), col(u_blk + 3), col(u_blk + 4), col(u_blk + 5),
            const((N_GROUPS_SGU, SGU_CHUNK, SGU_CHUNK)),
            const((SGU_CHUNK, D_SGU)),
            const((1, D_SGU)), const((1, D_SGU)),
            const((D_ATTN, d)), const((D_SGU, d)), const((d, d)),
            const((1, d)),
            const((d, LANES)), const((d, LANES)), const((1, LANES)),
        ],
        out_specs=[
            pl.BlockSpec((tm, d), lambda i: (i, 0)),
            pl.BlockSpec((tm, d), lambda i: (i, 0)),
            pl.BlockSpec((tm, LANES), lambda i: (i, 0)),
        ],
        out_shape=[
            jax.ShapeDtypeStruct((t, d), F32),
            jax.ShapeDtypeStruct((t, d), F32),
            jax.ShapeDtypeStruct((t, LANES), F32),
        ],
        scratch_shapes=[
            pltpu.VMEM((N_GROUPS_SGU, SGU_CHUNK, SGU_CHUNK), BF16),
            pltpu.VMEM((tm, D_SGU), BF16),
            pltpu.VMEM((tm, D_SGU), BF16),
            pltpu.VMEM((tm, d), BF16),
        ],
        compiler_params=pltpu.CompilerParams(
            dimension_semantics=("arbitrary",), vmem_limit_bytes=VMEM_LIMIT_BYTES),
        name="merge_router",
    )(x2, attn, proj, proj, proj, proj, proj, proj, sgu_w, sb_full,
      ln_g.reshape(1, -1), ln_b.reshape(1, -1), wpa, wpb, wo, g2.reshape(1, -1), wr_hi, wr_lo, br)


def _row_copy(src, si, dst, di, sem):
    return pltpu.make_async_copy(src.at[pl.ds(si, 1)], dst.at[pl.ds(di, 1)], sem)


def _dispatch_kernel(dest_ref, h2_hbm, xs_in_hbm, xs_hbm, sem):
    del xs_in_hbm
    tt = dest_ref.shape[2] // TOP_K_INNER
    base = pl.program_id(0) * tt

    def issue(t, carry):
        for k in range(TOP_K_INNER):
            _row_copy(h2_hbm, base + t, xs_hbm, dest_ref[0, 0, TOP_K_INNER * t + k], sem).start()
        return carry

    lax.fori_loop(0, tt, issue, 0)

    def drain(t, carry):
        for k in range(TOP_K_INNER):
            _row_copy(h2_hbm, 0, xs_hbm, 0, sem).wait()
        return carry

    lax.fori_loop(0, tt, drain, 0)


def _dispatch(dest3, h2, n_rows):
    t, d = h2.shape
    tt = dest3.shape[2] // TOP_K_INNER
    xs0 = jnp.zeros((n_rows, d), h2.dtype)
    return pl.pallas_call(
        _dispatch_kernel,
        grid=(t // tt,),
        in_specs=[
            pl.BlockSpec((1, 1, TOP_K_INNER * tt), lambda i: (i, 0, 0), memory_space=pltpu.SMEM),
            pl.BlockSpec(memory_space=pl.ANY),
            pl.BlockSpec(memory_space=pl.ANY),
        ],
        out_specs=pl.BlockSpec(memory_space=pl.ANY),
        out_shape=jax.ShapeDtypeStruct((n_rows, d), h2.dtype),
        scratch_shapes=[pltpu.SemaphoreType.DMA(())],
        input_output_aliases={2: 0},
        compiler_params=pltpu.CompilerParams(dimension_semantics=("arbitrary",)),
        name="moe_dispatch",
    )(dest3, h2, xs0)


def _expert_kernel(be_ref, nu_ref, x_ref, wg_ref, wu_ref, wd_ref, y_ref):
    del be_ref
    used = pl.program_id(0) < nu_ref[0]

    @pl.when(used)
    def _():
        h = x_ref[...].astype(BF16)
        g = _dot(h, wg_ref[0])
        u = _dot(h, wu_ref[0])
        hid = (g * _sigmoid(g) * u).astype(BF16)
        y_ref[...] = _dot(hid, wd_ref[0])

    @pl.when(jnp.logical_not(used))
    def _():
        y_ref[...] = jnp.zeros_like(y_ref)


def _experts(blk_e, n_used, xs, wg, wu, wd):
    n_rows, d = xs.shape
    ff = wg.shape[2]
    n_blocks = n_rows // DISPATCH_BLOCK
    grid_spec = pltpu.PrefetchScalarGridSpec(
        num_scalar_prefetch=2,
        grid=(n_blocks,),
        in_specs=[
            pl.BlockSpec((DISPATCH_BLOCK, d), lambda i, be, nu: (i, 0)),
            pl.BlockSpec((1, d, ff), lambda i, be, nu: (be[i], 0, 0)),
            pl.BlockSpec((1, d, ff), lambda i, be, nu: (be[i], 0, 0)),
            pl.BlockSpec((1, ff, d), lambda i, be, nu: (be[i], 0, 0)),
        ],
        out_specs=pl.BlockSpec((DISPATCH_BLOCK, d), lambda i, be, nu: (i, 0)),
    )
    return pl.pallas_call(
        _expert_kernel,
        grid_spec=grid_spec,
        out_shape=jax.ShapeDtypeStruct((n_rows, d), F32),
        compiler_params=pltpu.CompilerParams(
            dimension_semantics=("arbitrary",), vmem_limit_bytes=VMEM_LIMIT_BYTES),
        name="moe_experts",
    )(blk_e, n_used, xs, wg, wu, wd)


def _combine_kernel(dest_ref, x1_ref, rt_ref, gf_ref, y_hbm, o_ref, ybuf, sem):
    tt = x1_ref.shape[0]

    def issue(t, carry):
        for k in range(TOP_K_INNER):
            _row_copy(y_hbm, dest_ref[0, 0, TOP_K_INNER * t + k], ybuf.at[k], t, sem).start()
        return carry

    lax.fori_loop(0, tt, issue, 0)

    def drain(t, carry):
        for k in range(TOP_K_INNER):
            _row_copy(y_hbm, 0, ybuf.at[k], 0, sem).wait()
        return carry

    lax.fori_loop(0, tt, drain, 0)

    rt = rt_ref[...]
    w1 = rt[:, 2:3]
    w2 = rt[:, 3:4]
    z = x1_ref[...] + (w1 * ybuf[0] + w2 * ybuf[1])
    o_ref[...] = z * lax.rsqrt(jnp.mean(z * z, axis=-1, keepdims=True) + EPS) * gf_ref[...]


def _combine(dest3, x1, rt, gf, y):
    t, d = x1.shape
    tt = dest3.shape[2] // TOP_K_INNER
    return pl.pallas_call(
        _combine_kernel,
        grid=(t // tt,),
        in_specs=[
            pl.BlockSpec((1, 1, TOP_K_INNER * tt), lambda i: (i, 0, 0), memory_space=pltpu.SMEM),
            pl.BlockSpec((tt, d), lambda i: (i, 0)),
            pl.BlockSpec((tt, LANES), lambda i: (i, 0)),
            pl.BlockSpec((1, d), lambda i: (0, 0)),
            pl.BlockSpec(memory_space=pl.ANY),
        ],
        out_specs=pl.BlockSpec((tt, d), lambda i: (i, 0)),
        out_shape=jax.ShapeDtypeStruct((t, d), F32),
        scratch_shapes=[pltpu.VMEM((TOP_K_INNER, tt, d), F32), pltpu.SemaphoreType.DMA(())],
        compiler_params=pltpu.CompilerParams(
            dimension_semantics=("arbitrary",), vmem_limit_bytes=VMEM_LIMIT_BYTES),
        name="moe_combine",
    )(dest3, x1, rt, gf.reshape(1, d), y)


def _dispatch_plan(experts, t):
    n_assign = t * TOP_K_INNER
    flat_e = experts.reshape(n_assign)
    onehot = (flat_e[:, None] == jnp.arange(N_EXPERTS, dtype=jnp.int32)[None, :]).astype(jnp.int32)
    csum = jnp.cumsum(onehot, axis=0)
    counts = csum[-1]
    rank = jnp.sum(onehot * csum, axis=1) - 1
    padded = (counts + DISPATCH_BLOCK - 1) // DISPATCH_BLOCK * DISPATCH_BLOCK
    pad_end = jnp.cumsum(padded)
    pad_start = pad_end - padded
    dest = jnp.sum(onehot * pad_start[None, :], axis=1) + rank
    n_rows = -(-n_assign // DISPATCH_BLOCK) * DISPATCH_BLOCK + N_EXPERTS * DISPATCH_BLOCK
    n_blocks = n_rows // DISPATCH_BLOCK
    blk_start = jnp.arange(n_blocks, dtype=jnp.int32) * DISPATCH_BLOCK
    blk_e = jnp.minimum(jnp.sum((pad_end[None, :] <= blk_start[:, None]).astype(jnp.int32), axis=1),
                        N_EXPERTS - 1).astype(jnp.int32)
    n_used = (pad_end[-1] // DISPATCH_BLOCK).astype(jnp.int32).reshape(1)
    return dest.astype(jnp.int32), blk_e, n_used, n_rows


def _layer(x2, b, s, norm_mix_g, w_in, sgu_w, sgu_b, sgu_ln_g, sgu_ln_b, w_proj_attn, w_proj_sgu, w_out,
           norm_ffn_g, w_rg, b_rg, w_re, b_re, w_exp_gate, w_exp_up, w_exp_down):
    t, d = x2.shape
    proj = _inproj(x2, norm_mix_g, w_in.astype(BF16))
    attn = _attention(proj, b, s)

    n_r = N_EXPERT_GROUPS + N_EXPERTS
    w_r = jnp.concatenate([w_rg, w_re, jnp.zeros((d, LANES - n_r), F32)], axis=1)
    wr_hi = w_r.astype(BF16)
    wr_lo = (w_r - wr_hi.astype(F32)).astype(BF16)
    b_r = jnp.concatenate([b_rg, b_re, jnp.zeros((LANES - n_r,), F32)]).reshape(1, LANES)
    x1, h2, rt = _merge(x2, attn, proj, sgu_w, sgu_b, sgu_ln_g, sgu_ln_b,
                        w_proj_attn.astype(BF16), w_proj_sgu.astype(BF16), w_out.astype(BF16),
                        norm_ffn_g, wr_hi, wr_lo, b_r)

    experts = rt[:, 0:TOP_K_INNER].astype(jnp.int32)
    dest, blk_e, n_used, n_rows = _dispatch_plan(experts, t)
    tt = min(ROW_TILE, t)
    dest3 = dest.reshape(t // tt, 1, TOP_K_INNER * tt)
    xs = _dispatch(dest3, h2, n_rows)
    y = _experts(blk_e, n_used, xs, w_exp_gate.astype(BF16), w_exp_up.astype(BF16), w_exp_down.astype(BF16))
    return dest3, x1, rt, y


def kernel(x, norm_mix_g, w_in, sgu_w, sgu_b, sgu_ln_g, sgu_ln_b, w_proj_attn, w_proj_sgu, w_out, norm_ffn_g,
           w_router_group, b_router_group, w_router_expert, b_router_expert, w_exp_gate, w_exp_up, w_exp_down,
           norm_final_g):
    b, s, d = x.shape
    assert w_in.shape[0] == 1, "the combine kernel fuses the closing RMSNorm, so exactly one layer is supported"
    x2 = x.reshape(b * s, d)
    dest3, x1, rt, y = _layer(
        x2, b, s, norm_mix_g[0], w_in[0], sgu_w[0], sgu_b[0], sgu_ln_g[0], sgu_ln_b[0],
        w_proj_attn[0], w_proj_sgu[0], w_out[0], norm_ffn_g[0],
        w_router_group[0], b_router_group[0], w_router_expert[0], b_router_expert[0],
        w_exp_gate[0], w_exp_up[0], w_exp_down[0])
    return _combine(dest3, x1, rt, norm_final_g, y).reshape(b, s, d)
```

```python
import functools

import jax
import jax.numpy as jnp
from jax import lax
from jax.experimental import pallas as pl
from jax.experimental.pallas import tpu as pltpu

F32 = jnp.float32
BF16 = jnp.bfloat16

N_HEADS = 8
HEAD_DIM = 128
D_ATTN = N_HEADS * HEAD_DIM
MOBA_BLOCK = 256
MOBA_TOPK = 3
ROPE_THETA = 500000.0
ROPE_DIM = HEAD_DIM // 4
ROPE_HALF = ROPE_DIM // 2
N_GROUPS_SGU = 8
SGU_GROUP_DIM = 128
D_SGU = N_GROUPS_SGU * SGU_GROUP_DIM
SGU_CHUNK = 128
N_EXPERT_GROUPS = 4
EXPERTS_PER_GROUP = 8
N_EXPERTS = N_EXPERT_GROUPS * EXPERTS_PER_GROUP
TOP_K_INNER = 2
DISPATCH_BLOCK = 256
EPS = 1e-6

LANES = 128
VMEM_LIMIT_BYTES = 56 * 1024 * 1024
LOG2_E = 1.4426950408889634
MASK_BIAS = -1e30

INPROJ_TM = 1024
INPROJ_TN = 1024
MERGE_TM = 256
ROW_TILE = 256


def _dot(a, b):
    return jnp.dot(a, b, preferred_element_type=F32)


def _dot_nt(a, b):
    return lax.dot_general(a, b, (((1,), (1,)), ((), ())), preferred_element_type=F32)


def _sigmoid(x):
    return 1.0 / (1.0 + jnp.exp(-x))


def _gelu_tanh(x):
    return 0.5 * x * (1.0 + jnp.tanh(0.7978845608028654 * (x + 0.044715 * (x * x * x))))


def _inproj_kernel(x_ref, g_ref, w_ref, o_ref, h_ref):
    @pl.when(pl.program_id(1) == 0)
    def _():
        x = x_ref[...]
        ms = jnp.mean(x * x, axis=-1, keepdims=True)
        h_ref[...] = (x * lax.rsqrt(ms + EPS) * g_ref[...]).astype(BF16)

    o_ref[...] = _dot(h_ref[...], w_ref[...]).astype(o_ref.dtype)


def _inproj(x2, g, w_bf16):
    t, d = x2.shape
    n = w_bf16.shape[1]
    tm, tn = min(INPROJ_TM, t), INPROJ_TN
    return pl.pallas_call(
        _inproj_kernel,
        grid=(t // tm, n // tn),
        in_specs=[
            pl.BlockSpec((tm, d), lambda i, j: (i, 0)),
            pl.BlockSpec((1, d), lambda i, j: (0, 0)),
            pl.BlockSpec((d, tn), lambda i, j: (0, j)),
        ],
        out_specs=pl.BlockSpec((tm, tn), lambda i, j: (i, j)),
        out_shape=jax.ShapeDtypeStruct((t, n), BF16),
        scratch_shapes=[pltpu.VMEM((tm, d), BF16)],
        compiler_params=pltpu.CompilerParams(
            dimension_semantics=("parallel", "arbitrary"), vmem_limit_bytes=VMEM_LIMIT_BYTES),
        name="inproj",
    )(x2, g.reshape(1, d), w_bf16)


def _attn_kernel(q_ref, k_ref, v_ref, c_ref, s1_ref, s2_ref, o_ref, kaug_ref, km_ref, *, n_blk):
    blk = MOBA_BLOCK
    scale = HEAD_DIM ** -0.5 * LOG2_E
    lane = lax.broadcasted_iota(jnp.int32, (blk, LANES), 1)

    def rope(xb, r0):
        c = c_ref[pl.ds(r0, blk), :]
        s1 = s1_ref[pl.ds(r0, blk), :]
        s2 = s2_ref[pl.ds(r0, blk), :]
        return (xb * c + pltpu.roll(xb, LANES - ROPE_HALF, 1) * s1
                + pltpu.roll(xb, ROPE_HALF, 1) * s2)

    km_ref[...] = jnp.zeros_like(km_ref)

    def kprep(j, carry):
        r0 = pl.multiple_of(j * blk, blk)
        kr = rope(k_ref[pl.ds(r0, blk), :].astype(F32), r0)
        kaug_ref[pl.ds(r0, blk), 0:LANES] = kr.astype(BF16)
        kaug_ref[pl.ds(r0, blk), LANES:2 * LANES] = jnp.where(lane == j, 1.0, 0.0).astype(BF16)
        km_ref[pl.ds(j, 1), :] = jnp.mean(kr, axis=0, keepdims=True)
        return carry

    lax.fori_loop(0, n_blk, kprep, 0)

    row_i = lax.broadcasted_iota(jnp.int32, (blk, blk), 0)
    col_i = lax.broadcasted_iota(jnp.int32, (blk, blk), 1)

    for i in range(n_blk):
        r0 = i * blk
        qr = (rope(q_ref[r0:r0 + blk, :].astype(F32), r0) * scale).astype(BF16)
        km = km_ref[...]
        km_hi = km.astype(BF16)
        km_lo = (km - km_hi.astype(F32)).astype(BF16)
        gate = _dot_nt(qr, km_hi) + _dot_nt(qr, km_lo)
        g = jnp.where(lane < i, gate, -jnp.inf)
        sel = lane < 0
        for _ in range(MOBA_TOPK):
            mx = jnp.max(g, axis=-1, keepdims=True)
            idx = jnp.min(jnp.where(g == mx, lane, LANES), axis=-1, keepdims=True)
            hit = lane == idx
            sel = jnp.logical_or(sel, hit)
            g = jnp.where(hit, -jnp.inf, g)
        allowed = jnp.logical_or(jnp.logical_and(sel, lane < i), lane == i)
        bias = jnp.where(allowed, 0.0, MASK_BIAS).astype(BF16)
        qaug = jnp.concatenate([qr, bias], axis=1)

        s_own = _dot_nt(qaug, kaug_ref[r0:r0 + blk, :])
        s_own = jnp.where(col_i <= row_i, s_own, -jnp.inf)
        m = jnp.max(s_own, axis=-1, keepdims=True)
        if i > 0:
            s_past = _dot_nt(qaug, kaug_ref[0:r0, :])
            m = jnp.maximum(m, jnp.max(s_past, axis=-1, keepdims=True))
        p_own = jnp.exp2(s_own - m)
        l = jnp.sum(p_own, axis=-1, keepdims=True)
        acc = _dot(p_own.astype(BF16), v_ref[r0:r0 + blk, :])
        if i > 0:
            p_past = jnp.exp2(s_past - m)
            l = l + jnp.sum(p_past, axis=-1, keepdims=True)
            acc = acc + _dot(p_past.astype(BF16), v_ref[0:r0, :])
        o_ref[r0:r0 + blk, :] = (acc / l).astype(o_ref.dtype)


def _rope_tables(s):
    inv = ROPE_THETA ** (-jnp.arange(ROPE_HALF, dtype=F32) * 2.0 / ROPE_DIM)
    ang = jnp.arange(s).astype(F32)[:, None] * inv[None, :]
    cos, sin = jnp.cos(ang), jnp.sin(ang)
    zeros = jnp.zeros((s, LANES - ROPE_DIM), F32)
    zh = jnp.zeros((s, ROPE_HALF), F32)
    c = jnp.concatenate([cos, cos, zeros + 1.0], axis=1)
    s1 = jnp.concatenate([-sin, zh, zeros], axis=1)
    s2 = jnp.concatenate([zh, sin, zeros], axis=1)
    return c, s1, s2


def _attention(proj, b, s):
    assert s % MOBA_BLOCK == 0
    n_blk = s // MOBA_BLOCK
    assert n_blk <= LANES
    c, s1, s2 = _rope_tables(s)
    tab = pl.BlockSpec((s, LANES), lambda bi, h: (0, 0))
    return pl.pallas_call(
        functools.partial(_attn_kernel, n_blk=n_blk),
        grid=(b, N_HEADS),
        in_specs=[
            pl.BlockSpec((s, HEAD_DIM), lambda bi, h: (bi, h)),
            pl.BlockSpec((s, HEAD_DIM), lambda bi, h: (bi, N_HEADS + h)),
            pl.BlockSpec((s, HEAD_DIM), lambda bi, h: (bi, 2 * N_HEADS + h)),
            tab, tab, tab,
        ],
        out_specs=pl.BlockSpec((s, HEAD_DIM), lambda bi, h: (bi, h)),
        out_shape=jax.ShapeDtypeStruct((b * s, D_ATTN), BF16),
        scratch_shapes=[pltpu.VMEM((s, 2 * LANES), BF16), pltpu.VMEM((LANES, LANES), F32)],
        compiler_params=pltpu.CompilerParams(
            dimension_semantics=("parallel", "arbitrary"), vmem_limit_bytes=VMEM_LIMIT_BYTES),
        name="moba_attn",
    )(proj, proj, proj, c, s1, s2)


def _merge_kernel(x_ref, at_ref, u_ref, vg_ref, ga0_ref, ga1_ref, gs0_ref, gs1_ref,
                  sw_ref, sb_ref, lng_ref, lnb_ref, wpa_ref, wpb_ref, wo_ref, g2_ref,
                  wrh_ref, wrl_ref, br_ref,
                  x1_ref, h2_ref, rt_ref,
                  wm_ref, vgn_ref, sgu_ref, mg_ref):
    tm = x_ref.shape[0]
    d = x_ref.shape[1]
    half = d // 2

    @pl.when(pl.program_id(0) == 0)
    def _():
        r = lax.broadcasted_iota(jnp.int32, (SGU_CHUNK, SGU_CHUNK), 0)
        c = lax.broadcasted_iota(jnp.int32, (SGU_CHUNK, SGU_CHUNK), 1)
        for g in range(N_GROUPS_SGU):
            wm_ref[g] = jnp.where(c <= r, sw_ref[g], 0.0).astype(BF16)

    vg = _gelu_tanh(vg_ref[...].astype(F32))
    mu = jnp.mean(vg, axis=-1, keepdims=True)
    var = jnp.mean(jnp.square(vg - mu), axis=-1, keepdims=True)
    vgn_ref[...] = ((vg - mu) * lax.rsqrt(var + EPS) * lng_ref[...] + lnb_ref[...]).astype(BF16)
    for c in range(tm // SGU_CHUNK):
        rows = slice(c * SGU_CHUNK, (c + 1) * SGU_CHUNK)
        for g in range(N_GROUPS_SGU):
            cols = slice(g * SGU_GROUP_DIM, (g + 1) * SGU_GROUP_DIM)
            mixed = _dot(wm_ref[g], vgn_ref[rows, cols]) + sb_ref[:, cols]
            u = _gelu_tanh(u_ref[rows, cols].astype(F32))
            sgu_ref[rows, cols] = (u * mixed).astype(BF16)

    a = _dot(at_ref[...], wpa_ref[...])
    s = _dot(sgu_ref[...], wpb_ref[...])
    mg_ref[:, 0:half] = (_sigmoid(ga0_ref[...].astype(F32)) * a[:, 0:half]
                         + _sigmoid(gs0_ref[...].astype(F32)) * s[:, 0:half]).astype(BF16)
    mg_ref[:, half:d] = (_sigmoid(ga1_ref[...].astype(F32)) * a[:, half:d]
                         + _sigmoid(gs1_ref[...].astype(F32)) * s[:, half:d]).astype(BF16)
    x1 = x_ref[...] + _dot(mg_ref[...], wo_ref[...])
    x1_ref[...] = x1

    h2 = x1 * lax.rsqrt(jnp.mean(x1 * x1, axis=-1, keepdims=True) + EPS) * g2_ref[...]
    h2_ref[...] = h2
    hi = h2.astype(BF16)
    lo = (h2 - hi.astype(F32)).astype(BF16)
    logits = (_dot(hi, wrh_ref[...]) + _dot(lo, wrh_ref[...]) + _dot(hi, wrl_ref[...])) + br_ref[...]

    lane = lax.broadcasted_iota(jnp.int32, (tm, LANES), 1)
    neg = -jnp.inf
    gl = jnp.where(lane < N_EXPERT_GROUPS, logits, neg)
    gmax = jnp.max(gl, axis=-1, keepdims=True)
    grp = jnp.min(jnp.where(gl == gmax, lane, LANES), axis=-1, keepdims=True)
    p_grp = 1.0 / jnp.sum(jnp.exp(gl - gmax), axis=-1, keepdims=True)
    e0 = N_EXPERT_GROUPS + grp * EXPERTS_PER_GROUP
    emask = jnp.logical_and(lane >= e0, lane < e0 + EXPERTS_PER_GROUP)
    el = jnp.where(emask, logits, neg)
    emax = jnp.max(el, axis=-1, keepdims=True)
    ex = jnp.exp(el - emax)
    probs = jnp.where(emask, ex / jnp.sum(ex, axis=-1, keepdims=True), -1.0)
    p1 = jnp.max(probs, axis=-1, keepdims=True)
    i1 = jnp.min(jnp.where(probs == p1, lane, LANES), axis=-1, keepdims=True)
    probs2 = jnp.where(lane == i1, -1.0, probs)
    p2 = jnp.max(probs2, axis=-1, keepdims=True)
    i2 = jnp.min(jnp.where(probs2 == p2, lane, LANES), axis=-1, keepdims=True)
    den = p1 + p2
    w1 = p_grp * (p1 / den)
    w2 = p_grp * (p2 / den)
    ex1 = (i1 - N_EXPERT_GROUPS).astype(F32)
    ex2 = (i2 - N_EXPERT_GROUPS).astype(F32)
    rt_ref[...] = jnp.where(lane == 0, ex1, jnp.where(lane == 1, ex2,
                            jnp.where(lane == 2, w1, jnp.where(lane == 3, w2, 0.0))))


def _merge(x2, attn, proj, sgu_w, sgu_b, ln_g, ln_b, wpa, wpb, wo, g2, wr_hi, wr_lo, br):
    t, d = x2.shape
    tm = MERGE_TM
    assert d == 2 * D_ATTN and t % tm == 0
    u_blk = 3 * D_ATTN // D_SGU
    sb_full = jnp.repeat(sgu_b.T, SGU_GROUP_DIM, axis=1)

    def col(j):
        return pl.BlockSpec((tm, D_SGU), lambda i: (i, j))

    def const(shape):
        return pl.BlockSpec(shape, lambda i: tuple(0 for _ in shape))

    return pl.pallas_call(
        _merge_kernel,
        grid=(t // tm,),
        in_specs=[
            pl.BlockSpec((tm, d), lambda i: (i, 0)),
            pl.BlockSpec((tm, D_ATTN), lambda i: (i, 0)),
            col(u_blk), col(u_blk + 1), col(u_blk + 2), col(u_blk + 3), col(u_blk + 4), col(u_blk + 5),
            const((N_GROUPS_SGU, SGU_CHUNK, SGU_CHUNK)),
            const((SGU_CHUNK, D_SGU)),
            const((1, D_SGU)), const((1, D_SGU)),
            const((D_ATTN, d)), const((D_SGU, d)), const((d, d)),
            const((1, d)),
            const((d, LANES)), const((d, LANES)), const((1, LANES)),
        ],
        out_specs=[
            pl.BlockSpec((tm, d), lambda i: (i, 0)),
            pl.BlockSpec((tm, d), lambda i: (i, 0)),
            pl.BlockSpec((tm, LANES), lambda i: (i, 0)),
        ],
        out_shape=[
            jax.ShapeDtypeStruct((t, d), F32),
            jax.ShapeDtypeStruct((t, d), F32),
            jax.ShapeDtypeStruct((t, LANES), F32),
        ],
        scratch_shapes=[
            pltpu.VMEM((N_GROUPS_SGU, SGU_CHUNK, SGU_CHUNK), BF16),
            pltpu.VMEM((tm, D_SGU), BF16),
            pltpu.VMEM((tm, D_SGU), BF16),
            pltpu.VMEM((tm, d), BF16),
        ],
        compiler_params=pltpu.CompilerParams(
            dimension_semantics=("arbitrary",), vmem_limit_bytes=VMEM_LIMIT_BYTES),
        name="merge_router",
    )(x2, attn, proj, proj, proj, proj, proj, proj, sgu_w, sb_full,
      ln_g.reshape(1, -1), ln_b.reshape(1, -1), wpa, wpb, wo, g2.reshape(1, -1), wr_hi, wr_lo, br)


def _row_copy(src, si, dst, di, sem):
    return pltpu.make_async_copy(src.at[pl.ds(si, 1)], dst.at[pl.ds(di, 1)], sem)


def _gather_block_rows(tok_ref, src_hbm, dst, sem):
    def issue(r, carry):
        _row_copy(src_hbm, tok_ref[0, 0, r], dst, r, sem).start()
        return carry

    lax.fori_loop(0, DISPATCH_BLOCK, issue, 0, unroll=8)


def _wait_block_rows(src_hbm, dst, sem):
    def drain(r, carry):
        _row_copy(src_hbm, 0, dst, 0, sem).wait()
        return carry

    lax.fori_loop(0, DISPATCH_BLOCK, drain, 0, unroll=8)


def _expert_kernel(be_ref, nu_ref, tok_ref, tok_next_ref, h2_hbm, wg_ref, wu_ref, wd_ref, y_ref, xbuf, sems):
    del be_ref
    i = pl.program_id(0)
    n_used = nu_ref[0]
    slot = i % 2

    @pl.when(jnp.logical_and(i == 0, n_used > 0))
    def _():
        _gather_block_rows(tok_ref, h2_hbm, xbuf.at[0], sems.at[0])

    @pl.when(i + 1 < n_used)
    def _():
        _gather_block_rows(tok_next_ref, h2_hbm, xbuf.at[1 - slot], sems.at[1 - slot])

    @pl.when(i < n_used)
    def _():
        _wait_block_rows(h2_hbm, xbuf.at[slot], sems.at[slot])
        h = xbuf[slot].astype(BF16)
        g = _dot(h, wg_ref[0])
        u = _dot(h, wu_ref[0])
        hid = (g * _sigmoid(g) * u).astype(BF16)
        y_ref[...] = _dot(hid, wd_ref[0])

    @pl.when(i >= n_used)
    def _():
        y_ref[...] = jnp.zeros_like(y_ref)


def _experts(blk_e, n_used, row_tok, h2, wg, wu, wd):
    d = h2.shape[1]
    ff = wg.shape[2]
    n_blocks = row_tok.shape[0] // DISPATCH_BLOCK
    tok3 = row_tok.reshape(n_blocks, 1, DISPATCH_BLOCK)
    grid_spec = pltpu.PrefetchScalarGridSpec(
        num_scalar_prefetch=2,
        grid=(n_blocks,),
        in_specs=[
            pl.BlockSpec((1, 1, DISPATCH_BLOCK), lambda i, be, nu: (i, 0, 0), memory_space=pltpu.SMEM),
            pl.BlockSpec((1, 1, DISPATCH_BLOCK), lambda i, be, nu: (jnp.minimum(i + 1, n_blocks - 1), 0, 0),
                         memory_space=pltpu.SMEM),
            pl.BlockSpec(memory_space=pl.ANY),
            pl.BlockSpec((1, d, ff), lambda i, be, nu: (be[i], 0, 0)),
            pl.BlockSpec((1, d, ff), lambda i, be, nu: (be[i], 0, 0)),
            pl.BlockSpec((1, ff, d), lambda i, be, nu: (be[i], 0, 0)),
        ],
        out_specs=pl.BlockSpec((DISPATCH_BLOCK, d), lambda i, be, nu: (i, 0)),
        scratch_shapes=[pltpu.VMEM((2, DISPATCH_BLOCK, d), F32), pltpu.SemaphoreType.DMA((2,))],
    )
    return pl.pallas_call(
        _expert_kernel,
        grid_spec=grid_spec,
        out_shape=jax.ShapeDtypeStruct((n_blocks * DISPATCH_BLOCK, d), F32),
        compiler_params=pltpu.CompilerParams(
            dimension_semantics=("arbitrary",), vmem_limit_bytes=VMEM_LIMIT_BYTES),
        name="moe_experts",
    )(blk_e, n_used, tok3, tok3, h2, wg, wu, wd)


def _combine_kernel(dest_ref, x1_ref, rt_ref, gf_ref, y_hbm, o_ref, ybuf, sem):
    tt = x1_ref.shape[0]

    def issue(t, carry):
        for k in range(TOP_K_INNER):
            _row_copy(y_hbm, dest_ref[0, 0, TOP_K_INNER * t + k], ybuf.at[k], t, sem).start()
        return carry

    lax.fori_loop(0, tt, issue, 0)

    def drain(t, carry):
        for k in range(TOP_K_INNER):
            _row_copy(y_hbm, 0, ybuf.at[k], 0, sem).wait()
        return carry

    lax.fori_loop(0, tt, drain, 0)

    rt = rt_ref[...]
    w1 = rt[:, 2:3]
    w2 = rt[:, 3:4]
    z = x1_ref[...] + (w1 * ybuf[0] + w2 * ybuf[1])
    o_ref[...] = z * lax.rsqrt(jnp.mean(z * z, axis=-1, keepdims=True) + EPS) * gf_ref[...]


def _combine(dest3, x1, rt, gf, y):
    t, d = x1.shape
    tt = dest3.shape[2] // TOP_K_INNER
    return pl.pallas_call(
        _combine_kernel,
        grid=(t // tt,),
        in_specs=[
            pl.BlockSpec((1, 1, TOP_K_INNER * tt), lambda i: (i, 0, 0), memory_space=pltpu.SMEM),
            pl.BlockSpec((tt, d), lambda i: (i, 0)),
            pl.BlockSpec((tt, LANES), lambda i: (i, 0)),
            pl.BlockSpec((1, d), lambda i: (0, 0)),
            pl.BlockSpec(memory_space=pl.ANY),
        ],
        out_specs=pl.BlockSpec((tt, d), lambda i: (i, 0)),
        out_shape=jax.ShapeDtypeStruct((t, d), F32),
        scratch_shapes=[pltpu.VMEM((TOP_K_INNER, tt, d), F32), pltpu.SemaphoreType.DMA(())],
        compiler_params=pltpu.CompilerParams(
            dimension_semantics=("arbitrary",), vmem_limit_bytes=VMEM_LIMIT_BYTES),
        name="moe_combine",
    )(dest3, x1, rt, gf.reshape(1, d), y)


def _dispatch_plan(experts, t):
    n_assign = t * TOP_K_INNER
    flat_e = experts.reshape(n_assign)
    onehot = (flat_e[:, None] == jnp.arange(N_EXPERTS, dtype=jnp.int32)[None, :]).astype(jnp.int32)
    csum = jnp.cumsum(onehot, axis=0)
    counts = csum[-1]
    rank = jnp.sum(onehot * csum, axis=1) - 1
    padded = (counts + DISPATCH_BLOCK - 1) // DISPATCH_BLOCK * DISPATCH_BLOCK
    pad_end = jnp.cumsum(padded)
    pad_start = pad_end - padded
    dest = jnp.sum(onehot * pad_start[None, :], axis=1) + rank
    n_rows = -(-n_assign // DISPATCH_BLOCK) * DISPATCH_BLOCK + N_EXPERTS * DISPATCH_BLOCK
    n_blocks = n_rows // DISPATCH_BLOCK
    blk_start = jnp.arange(n_blocks, dtype=jnp.int32) * DISPATCH_BLOCK
    blk_e = jnp.minimum(jnp.sum((pad_end[None, :] <= blk_start[:, None]).astype(jnp.int32), axis=1),
                        N_EXPERTS - 1).astype(jnp.int32)
    n_used = (pad_end[-1] // DISPATCH_BLOCK).astype(jnp.int32).reshape(1)
    row_tok = jnp.zeros((n_rows,), jnp.int32).at[dest].set(
        jnp.arange(n_assign, dtype=jnp.int32) // TOP_K_INNER, unique_indices=True)
    return dest.astype(jnp.int32), row_tok, blk_e, n_used


def _layer(x2, b, s, norm_mix_g, w_in, sgu_w, sgu_b, sgu_ln_g, sgu_ln_b, w_proj_attn, w_proj_sgu, w_out,
           norm_ffn_g, w_rg, b_rg, w_re, b_re, w_exp_gate, w_exp_up, w_exp_down):
    t, d = x2.shape
    proj = _inproj(x2, norm_mix_g, w_in.astype(BF16))
    attn = _attention(proj, b, s)

    n_r = N_EXPERT_GROUPS + N_EXPERTS
    w_r = jnp.concatenate([w_rg, w_re, jnp.zeros((d, LANES - n_r), F32)], axis=1)
    wr_hi = w_r.astype(BF16)
    wr_lo = (w_r - wr_hi.astype(F32)).astype(BF16)
    b_r = jnp.concatenate([b_rg, b_re, jnp.zeros((LANES - n_r,), F32)]).reshape(1, LANES)
    x1, h2, rt = _merge(x2, attn, proj, sgu_w, sgu_b, sgu_ln_g, sgu_ln_b,
                        w_proj_attn.astype(BF16), w_proj_sgu.astype(BF16), w_out.astype(BF16),
                        norm_ffn_g, wr_hi, wr_lo, b_r)

    experts = rt[:, 0:TOP_K_INNER].astype(jnp.int32)
    dest, row_tok, blk_e, n_used = _dispatch_plan(experts, t)
    tt = min(ROW_TILE, t)
    dest3 = dest.reshape(t // tt, 1, TOP_K_INNER * tt)
    y = _experts(blk_e, n_used, row_tok, h2, w_exp_gate.astype(BF16), w_exp_up.astype(BF16), w_exp_down.astype(BF16))
    return dest3, x1, rt, y


def kernel(x, norm_mix_g, w_in, sgu_w, sgu_b, sgu_ln_g, sgu_ln_b, w_proj_attn, w_proj_sgu, w_out, norm_ffn_g,
           w_router_group, b_router_group, w_router_expert, b_router_expert, w_exp_gate, w_exp_up, w_exp_down,
           norm_final_g):
    b, s, d = x.shape
    assert w_in.shape[0] == 1, "the combine kernel fuses the closing RMSNorm, so exactly one layer is supported"
    x2 = x.reshape(b * s, d)
    dest3, x1, rt, y = _layer(
        x2, b, s, norm_mix_g[0], w_in[0], sgu_w[0], sgu_b[0], sgu_ln_g[0], sgu_ln_b[0],
        w_proj_attn[0], w_proj_sgu[0], w_out[0], norm_ffn_g[0],
        w_router_group[0], b_router_group[0], w_router_expert[0], b_router_expert[0],
        w_exp_gate[0], w_exp_up[0], w_exp_down[0])
    return _combine(dest3, x1, rt, norm_final_g, y).reshape(b, s, d)
```

```python
import functools

import jax
import jax.numpy as jnp
from jax import lax
from jax.experimental import pallas as pl
from jax.experimental.pallas import tpu as pltpu

F32 = jnp.float32
BF16 = jnp.bfloat16

N_HEADS = 8
HEAD_DIM = 128
D_ATTN = N_HEADS * HEAD_DIM
MOBA_BLOCK = 256
MOBA_TOPK = 3
ROPE_THETA = 500000.0
ROPE_DIM = HEAD_DIM // 4
ROPE_HALF = ROPE_DIM // 2
N_GROUPS_SGU = 8
SGU_GROUP_DIM = 128
D_SGU = N_GROUPS_SGU * SGU_GROUP_DIM
SGU_CHUNK = 128
N_EXPERT_GROUPS = 4
EXPERTS_PER_GROUP = 8
N_EXPERTS = N_EXPERT_GROUPS * EXPERTS_PER_GROUP
TOP_K_INNER = 2
DISPATCH_BLOCK = 256
EPS = 1e-6

LANES = 128
VMEM_LIMIT_BYTES = 56 * 1024 * 1024
LOG2_E = 1.4426950408889634
MASK_BIAS = -1e30

INPROJ_TM = 1024
INPROJ_TN = 1024
MERGE_TM = 256
ROW_TILE = 256


def _dot(a, b):
    return jnp.dot(a, b, preferred_element_type=F32)


def _dot_nt(a, b):
    return lax.dot_general(a, b, (((1,), (1,)), ((), ())), preferred_element_type=F32)


def _sigmoid(x):
    return 1.0 / (1.0 + jnp.exp(-x))


def _gelu_tanh(x):
    return 0.5 * x * (1.0 + jnp.tanh(0.7978845608028654 * (x + 0.044715 * (x * x * x))))


def _inproj_kernel(x_ref, g_ref, w_ref, o_ref, h_ref):
    @pl.when(pl.program_id(1) == 0)
    def _():
        x = x_ref[...]
        ms = jnp.mean(x * x, axis=-1, keepdims=True)
        h_ref[...] = (x * lax.rsqrt(ms + EPS) * g_ref[...]).astype(BF16)

    o_ref[...] = _dot(h_ref[...], w_ref[...]).astype(o_ref.dtype)


def _inproj(x2, g, w_bf16):
    t, d = x2.shape
    n = w_bf16.shape[1]
    tm, tn = min(INPROJ_TM, t), INPROJ_TN
    return pl.pallas_call(
        _inproj_kernel,
        grid=(t // tm, n // tn),
        in_specs=[
            pl.BlockSpec((tm, d), lambda i, j: (i, 0)),
            pl.BlockSpec((1, d), lambda i, j: (0, 0)),
            pl.BlockSpec((d, tn), lambda i, j: (0, j)),
        ],
        out_specs=pl.BlockSpec((tm, tn), lambda i, j: (i, j)),
        out_shape=jax.ShapeDtypeStruct((t, n), BF16),
        scratch_shapes=[pltpu.VMEM((tm, d), BF16)],
        compiler_params=pltpu.CompilerParams(
            dimension_semantics=("parallel", "arbitrary"), vmem_limit_bytes=VMEM_LIMIT_BYTES),
        name="inproj",
    )(x2, g.reshape(1, d), w_bf16)


def _attn_kernel(q_ref, k_ref, v_ref, c_ref, s1_ref, s2_ref, o_ref, kaug_ref, km_ref, s_ref, *, n_blk):
    blk = MOBA_BLOCK
    scale = HEAD_DIM ** -0.5 * LOG2_E
    lane = lax.broadcasted_iota(jnp.int32, (blk, LANES), 1)

    def rope(xb, r0):
        c = c_ref[pl.ds(r0, blk), :]
        s1 = s1_ref[pl.ds(r0, blk), :]
        s2 = s2_ref[pl.ds(r0, blk), :]
        return (xb * c + pltpu.roll(xb, LANES - ROPE_HALF, 1) * s1
                + pltpu.roll(xb, ROPE_HALF, 1) * s2)

    km_ref[...] = jnp.zeros_like(km_ref)

    def kprep(j, carry):
        r0 = pl.multiple_of(j * blk, blk)
        kr = rope(k_ref[pl.ds(r0, blk), :].astype(F32), r0)
        kaug_ref[pl.ds(r0, blk), 0:LANES] = kr.astype(BF16)
        kaug_ref[pl.ds(r0, blk), LANES:2 * LANES] = jnp.where(lane == j, 1.0, 0.0).astype(BF16)
        km_ref[pl.ds(j, 1), :] = jnp.mean(kr, axis=0, keepdims=True)
        return carry

    lax.fori_loop(0, n_blk, kprep, 0)

    row_i = lax.broadcasted_iota(jnp.int32, (blk, blk), 0)
    col_i = lax.broadcasted_iota(jnp.int32, (blk, blk), 1)

    def make_qaug(i):
        r0 = i * blk
        qr = (rope(q_ref[r0:r0 + blk, :].astype(F32), r0) * scale).astype(BF16)
        km = km_ref[...]
        km_hi = km.astype(BF16)
        km_lo = (km - km_hi.astype(F32)).astype(BF16)
        gate = _dot_nt(qr, km_hi) + _dot_nt(qr, km_lo)
        g = jnp.where(lane < i, gate, -jnp.inf)
        sel = lane < 0
        for _ in range(MOBA_TOPK):
            mx = jnp.max(g, axis=-1, keepdims=True)
            idx = jnp.min(jnp.where(g == mx, lane, LANES), axis=-1, keepdims=True)
            hit = lane == idx
            sel = jnp.logical_or(sel, hit)
            g = jnp.where(hit, -jnp.inf, g)
        allowed = jnp.logical_or(jnp.logical_and(sel, lane < i), lane == i)
        bias = jnp.where(allowed, 0.0, MASK_BIAS).astype(BF16)
        return jnp.concatenate([qr, bias], axis=1)

    def score_chunk(i, qaug, c, fold_max):
        s = _dot_nt(qaug, kaug_ref[c * blk:(c + 1) * blk, :])
        if c == i:
            s = jnp.where(col_i <= row_i, s, -jnp.inf)
        s_ref[i % 2, :, c * blk:(c + 1) * blk] = s
        f = jnp.maximum(s[:, 0:LANES], s[:, LANES:2 * LANES])
        return f if fold_max is None else jnp.maximum(fold_max, f)

    def value_chunk(i, m, c, fold_sum, acc):
        p = jnp.exp2(s_ref[i % 2, :, c * blk:(c + 1) * blk] - m)
        f = p[:, 0:LANES] + p[:, LANES:2 * LANES]
        pv = _dot(p.astype(BF16), v_ref[c * blk:(c + 1) * blk, :])
        return (f, pv) if acc is None else (fold_sum + f, acc + pv)

    qaug = make_qaug(0)
    m_cur = jnp.max(score_chunk(0, qaug, 0, None), axis=-1, keepdims=True)
    for i in range(n_blk):
        has_next = i + 1 < n_blk
        n_a = i + 2 if has_next else 0
        n_b = i + 1
        if has_next:
            qaug = make_qaug(i + 1)
        fold_max = fold_sum = acc = None
        for c in range(max(n_a, n_b)):
            if c < n_a:
                fold_max = score_chunk(i + 1, qaug, c, fold_max)
            if c < n_b:
                fold_sum, acc = value_chunk(i, m_cur, c, fold_sum, acc)
        l = jnp.sum(fold_sum, axis=-1, keepdims=True)
        o_ref[i * blk:(i + 1) * blk, :] = (acc / l).astype(o_ref.dtype)
        if has_next:
            m_cur = jnp.max(fold_max, axis=-1, keepdims=True)


def _rope_tables(s):
    inv = ROPE_THETA ** (-jnp.arange(ROPE_HALF, dtype=F32) * 2.0 / ROPE_DIM)
    ang = jnp.arange(s).astype(F32)[:, None] * inv[None, :]
    cos, sin = jnp.cos(ang), jnp.sin(ang)
    zeros = jnp.zeros((s, LANES - ROPE_DIM), F32)
    zh = jnp.zeros((s, ROPE_HALF), F32)
    c = jnp.concatenate([cos, cos, zeros + 1.0], axis=1)
    s1 = jnp.concatenate([-sin, zh, zeros], axis=1)
    s2 = jnp.concatenate([zh, sin, zeros], axis=1)
    return c, s1, s2


def _attention(proj, b, s):
    assert s % MOBA_BLOCK == 0
    n_blk = s // MOBA_BLOCK
    assert n_blk <= LANES
    c, s1, s2 = _rope_tables(s)
    tab = pl.BlockSpec((s, LANES), lambda bi, h: (0, 0))
    return pl.pallas_call(
        functools.partial(_attn_kernel, n_blk=n_blk),
        grid=(b, N_HEADS),
        in_specs=[
            pl.BlockSpec((s, HEAD_DIM), lambda bi, h: (bi, h)),
            pl.BlockSpec((s, HEAD_DIM), lambda bi, h: (bi, N_HEADS + h)),
            pl.BlockSpec((s, HEAD_DIM), lambda bi, h: (bi, 2 * N_HEADS + h)),
            tab, tab, tab,
        ],
        out_specs=pl.BlockSpec((s, HEAD_DIM), lambda bi, h: (bi, h)),
        out_shape=jax.ShapeDtypeStruct((b * s, D_ATTN), BF16),
        scratch_shapes=[pltpu.VMEM((s, 2 * LANES), BF16), pltpu.VMEM((LANES, LANES), F32),
                        pltpu.VMEM((2, MOBA_BLOCK, s), F32)],
        compiler_params=pltpu.CompilerParams(
            dimension_semantics=("parallel", "arbitrary"), vmem_limit_bytes=VMEM_LIMIT_BYTES),
        name="moba_attn",
    )(proj, proj, proj, c, s1, s2)


def _merge_kernel(x_ref, at_ref, u_ref, vg_ref, ga0_ref, ga1_ref, gs0_ref, gs1_ref,
                  sw_ref, sb_ref, lng_ref, lnb_ref, wpa_ref, wpb_ref, wo_ref, g2_ref,
                  wrh_ref, wrl_ref, br_ref,
                  x1_ref, h2_ref, rt_ref,
                  wm_ref, vgn_ref, sgu_ref, mg_ref):
    tm = x_ref.shape[0]
    d = x_ref.shape[1]
    half = d // 2

    @pl.when(pl.program_id(0) == 0)
    def _():
        r = lax.broadcasted_iota(jnp.int32, (SGU_CHUNK, SGU_CHUNK), 0)
        c = lax.broadcasted_iota(jnp.int32, (SGU_CHUNK, SGU_CHUNK), 1)
        for g in range(N_GROUPS_SGU):
            wm_ref[g] = jnp.where(c <= r, sw_ref[g], 0.0).astype(BF16)

    vg = _gelu_tanh(vg_ref[...].astype(F32))
    mu = jnp.mean(vg, axis=-1, keepdims=True)
    var = jnp.mean(jnp.square(vg - mu), axis=-1, keepdims=True)
    vgn_ref[...] = ((vg - mu) * lax.rsqrt(var + EPS) * lng_ref[...] + lnb_ref[...]).astype(BF16)
    for c in range(tm // SGU_CHUNK):
        rows = slice(c * SGU_CHUNK, (c + 1) * SGU_CHUNK)
        for g in range(N_GROUPS_SGU):
            cols = slice(g * SGU_GROUP_DIM, (g + 1) * SGU_GROUP_DIM)
            mixed = _dot(wm_ref[g], vgn_ref[rows, cols]) + sb_ref[:, cols]
            u = _gelu_tanh(u_ref[rows, cols].astype(F32))
            sgu_ref[rows, cols] = (u * mixed).astype(BF16)

    a = _dot(at_ref[...], wpa_ref[...])
    s = _dot(sgu_ref[...], wpb_ref[...])
    mg_ref[:, 0:half] = (_sigmoid(ga0_ref[...].astype(F32)) * a[:, 0:half]
                         + _sigmoid(gs0_ref[...].astype(F32)) * s[:, 0:half]).astype(BF16)
    mg_ref[:, half:d] = (_sigmoid(ga1_ref[...].astype(F32)) * a[:, half:d]
                         + _sigmoid(gs1_ref[...].astype(F32)) * s[:, half:d]).astype(BF16)
    x1 = x_ref[...] + _dot(mg_ref[...], wo_ref[...])
    x1_ref[...] = x1

    h2 = x1 * lax.rsqrt(jnp.mean(x1 * x1, axis=-1, keepdims=True) + EPS) * g2_ref[...]
    h2_ref[...] = h2
    hi = h2.astype(BF16)
    lo = (h2 - hi.astype(F32)).astype(BF16)
    logits = (_dot(hi, wrh_ref[...]) + _dot(lo, wrh_ref[...]) + _dot(hi, wrl_ref[...])) + br_ref[...]

    lane = lax.broadcasted_iota(jnp.int32, (tm, LANES), 1)
    neg = -jnp.inf
    gl = jnp.where(lane < N_EXPERT_GROUPS, logits, neg)
    gmax = jnp.max(gl, axis=-1, keepdims=True)
    grp = jnp.min(jnp.where(gl == gmax, lane, LANES), axis=-1, keepdims=True)
    p_grp = 1.0 / jnp.sum(jnp.exp(gl - gmax), axis=-1, keepdims=True)
    e0 = N_EXPERT_GROUPS + grp * EXPERTS_PER_GROUP
    emask = jnp.logical_and(lane >= e0, lane < e0 + EXPERTS_PER_GROUP)
    el = jnp.where(emask, logits, neg)
    emax = jnp.max(el, axis=-1, keepdims=True)
    ex = jnp.exp(el - emax)
    probs = jnp.where(emask, ex / jnp.sum(ex, axis=-1, keepdims=True), -1.0)
    p1 = jnp.max(probs, axis=-1, keepdims=True)
    i1 = jnp.min(jnp.where(probs == p1, lane, LANES), axis=-1, keepdims=True)
    probs2 = jnp.where(lane == i1, -1.0, probs)
    p2 = jnp.max(probs2, axis=-1, keepdims=True)
    i2 = jnp.min(jnp.where(probs2 == p2, lane, LANES), axis=-1, keepdims=True)
    den = p1 + p2
    w1 = p_grp * (p1 / den)
    w2 = p_grp * (p2 / den)
    ex1 = (i1 - N_EXPERT_GROUPS).astype(F32)
    ex2 = (i2 - N_EXPERT_GROUPS).astype(F32)
    rt_ref[...] = jnp.where(lane == 0, ex1, jnp.where(lane == 1, ex2,
                            jnp.where(lane == 2, w1, jnp.where(lane == 3, w2, 0.0))))


def _merge(x2, attn, proj, sgu_w, sgu_b, ln_g, ln_b, wpa, wpb, wo, g2, wr_hi, wr_lo, br):
    t, d = x2.shape
    tm = MERGE_TM
    assert d == 2 * D_ATTN and t % tm == 0
    u_blk = 3 * D_ATTN // D_SGU
    sb_full = jnp.repeat(sgu_b.T, SGU_GROUP_DIM, axis=1)

    def col(j):
        return pl.BlockSpec((tm, D_SGU), lambda i: (i, j))

    def const(shape):
        return pl.BlockSpec(shape, lambda i: tuple(0 for _ in shape))

    return pl.pallas_call(
        _merge_kernel,
        grid=(t // tm,),
        in_specs=[
            pl.BlockSpec((tm, d), lambda i: (i, 0)),
            pl.BlockSpec((tm, D_ATTN), lambda i: (i, 0)),
            col(u_blk), col(u_blk + 1), col(u_blk + 2), col(u_blk + 3), col(u_blk + 4), col(u_blk + 5),
            const((N_GROUPS_SGU, SGU_CHUNK, SGU_CHUNK)),
            const((SGU_CHUNK, D_SGU)),
            const((1, D_SGU)), const((1, D_SGU)),
            const((D_ATTN, d)), const((D_SGU, d)), const((d, d)),
            const((1, d)),
            const((d, LANES)), const((d, LANES)), const((1, LANES)),
        ],
        out_specs=[
            pl.BlockSpec((tm, d), lambda i: (i, 0)),
            pl.BlockSpec((tm, d), lambda i: (i, 0)),
            pl.BlockSpec((tm, LANES), lambda i: (i, 0)),
        ],
        out_shape=[
            jax.ShapeDtypeStruct((t, d), F32),
            jax.ShapeDtypeStruct((t, d), F32),
            jax.ShapeDtypeStruct((t, LANES), F32),
        ],
        scratch_shapes=[
            pltpu.VMEM((N_GROUPS_SGU, SGU_CHUNK, SGU_CHUNK), BF16),
            pltpu.VMEM((tm, D_SGU), BF16),
            pltpu.VMEM((tm, D_SGU), BF16),
            pltpu.VMEM((tm, d), BF16),
        ],
        compiler_params=pltpu.CompilerParams(
            dimension_semantics=("arbitrary",), vmem_limit_bytes=VMEM_LIMIT_BYTES),
        name="merge_router",
    )(x2, attn, proj, proj, proj, proj, proj, proj, sgu_w, sb_full,
      ln_g.reshape(1, -1), ln_b.reshape(1, -1), wpa, wpb, wo, g2.reshape(1, -1), wr_hi, wr_lo, br)


def _row_copy(src, si, dst, di, sem):
    return pltpu.make_async_copy(src.at[pl.ds(si, 1)], dst.at[pl.ds(di, 1)], sem)


def _gather_block_rows(tok_ref, src_hbm, dst, sem):
    def issue(r, carry):
        _row_copy(src_hbm, tok_ref[0, 0, r], dst, r, sem).start()
        return carry

    lax.fori_loop(0, DISPATCH_BLOCK, issue, 0, unroll=8)


def _wait_block_rows(src_hbm, dst, sem):
    def drain(r, carry):
        _row_copy(src_hbm, 0, dst, 0, sem).wait()
        return carry

    lax.fori_loop(0, DISPATCH_BLOCK, drain, 0, unroll=8)


def _expert_kernel(be_ref, nu_ref, tok_ref, tok_next_ref, h2_hbm, wg_ref, wu_ref, wd_ref, y_ref, xbuf, sems):
    del be_ref
    i = pl.program_id(0)
    n_used = nu_ref[0]
    slot = i % 2

    @pl.when(jnp.logical_and(i == 0, n_used > 0))
    def _():
        _gather_block_rows(tok_ref, h2_hbm, xbuf.at[0], sems.at[0])

    @pl.when(i + 1 < n_used)
    def _():
        _gather_block_rows(tok_next_ref, h2_hbm, xbuf.at[1 - slot], sems.at[1 - slot])

    @pl.when(i < n_used)
    def _():
        _wait_block_rows(h2_hbm, xbuf.at[slot], sems.at[slot])
        h = xbuf[slot].astype(BF16)
        g = _dot(h, wg_ref[0])
        u = _dot(h, wu_ref[0])
        hid = (g * _sigmoid(g) * u).astype(BF16)
        y_ref[...] = _dot(hid, wd_ref[0])

    @pl.when(i >= n_used)
    def _():
        y_ref[...] = jnp.zeros_like(y_ref)


def _experts(blk_e, n_used, row_tok, h2, wg, wu, wd):
    d = h2.shape[1]
    ff = wg.shape[2]
    n_blocks = row_tok.shape[0] // DISPATCH_BLOCK
    tok3 = row_tok.reshape(n_blocks, 1, DISPATCH_BLOCK)
    grid_spec = pltpu.PrefetchScalarGridSpec(
        num_scalar_prefetch=2,
        grid=(n_blocks,),
        in_specs=[
            pl.BlockSpec((1, 1, DISPATCH_BLOCK), lambda i, be, nu: (i, 0, 0), memory_space=pltpu.SMEM),
            pl.BlockSpec((1, 1, DISPATCH_BLOCK), lambda i, be, nu: (jnp.minimum(i + 1, n_blocks - 1), 0, 0),
                         memory_space=pltpu.SMEM),
            pl.BlockSpec(memory_space=pl.ANY),
            pl.BlockSpec((1, d, ff), lambda i, be, nu: (be[i], 0, 0)),
            pl.BlockSpec((1, d, ff), lambda i, be, nu: (be[i], 0, 0)),
            pl.BlockSpec((1, ff, d), lambda i, be, nu: (be[i], 0, 0)),
        ],
        out_specs=pl.BlockSpec((DISPATCH_BLOCK, d), lambda i, be, nu: (i, 0)),
        scratch_shapes=[pltpu.VMEM((2, DISPATCH_BLOCK, d), F32), pltpu.SemaphoreType.DMA((2,))],
    )
    return pl.pallas_call(
        _expert_kernel,
        grid_spec=grid_spec,
        out_shape=jax.ShapeDtypeStruct((n_blocks * DISPATCH_BLOCK, d), F32),
        compiler_params=pltpu.CompilerParams(
            dimension_semantics=("arbitrary",), vmem_limit_bytes=VMEM_LIMIT_BYTES),
        name="moe_experts",
    )(blk_e, n_used, tok3, tok3, h2, wg, wu, wd)


def _combine_kernel(dest_ref, x1_ref, rt_ref, gf_ref, y_hbm, o_ref, ybuf, sem):
    tt = x1_ref.shape[0]

    def issue(t, carry):
        for k in range(TOP_K_INNER):
            _row_copy(y_hbm, dest_ref[0, 0, TOP_K_INNER * t + k], ybuf.at[k], t, sem).start()
        return carry

    lax.fori_loop(0, tt, issue, 0)

    def drain(t, carry):
        for k in range(TOP_K_INNER):
            _row_copy(y_hbm, 0, ybuf.at[k], 0, sem).wait()
        return carry

    lax.fori_loop(0, tt, drain, 0)

    rt = rt_ref[...]
    w1 = rt[:, 2:3]
    w2 = rt[:, 3:4]
    z = x1_ref[...] + (w1 * ybuf[0] + w2 * ybuf[1])
    o_ref[...] = z * lax.rsqrt(jnp.mean(z * z, axis=-1, keepdims=True) + EPS) * gf_ref[...]


def _combine(dest3, x1, rt, gf, y):
    t, d = x1.shape
    tt = dest3.shape[2] // TOP_K_INNER
    return pl.pallas_call(
        _combine_kernel,
        grid=(t // tt,),
        in_specs=[
            pl.BlockSpec((1, 1, TOP_K_INNER * tt), lambda i: (i, 0, 0), memory_space=pltpu.SMEM),
            pl.BlockSpec((tt, d), lambda i: (i, 0)),
            pl.BlockSpec((tt, LANES), lambda i: (i, 0)),
            pl.BlockSpec((1, d), lambda i: (0, 0)),
            pl.BlockSpec(memory_space=pl.ANY),
        ],
        out_specs=pl.BlockSpec((tt, d), lambda i: (i, 0)),
        out_shape=jax.ShapeDtypeStruct((t, d), F32),
        scratch_shapes=[pltpu.VMEM((TOP_K_INNER, tt, d), F32), pltpu.SemaphoreType.DMA(())],
        compiler_params=pltpu.CompilerParams(
            dimension_semantics=("arbitrary",), vmem_limit_bytes=VMEM_LIMIT_BYTES),
        name="moe_combine",
    )(dest3, x1, rt, gf.reshape(1, d), y)


def _dispatch_plan(experts, t):
    n_assign = t * TOP_K_INNER
    flat_e = experts.reshape(n_assign)
    onehot = (flat_e[:, None] == jnp.arange(N_EXPERTS, dtype=jnp.int32)[None, :]).astype(jnp.int32)
    csum = jnp.cumsum(onehot, axis=0)
    counts = csum[-1]
    rank = jnp.sum(onehot * csum, axis=1) - 1
    padded = (counts + DISPATCH_BLOCK - 1) // DISPATCH_BLOCK * DISPATCH_BLOCK
    pad_end = jnp.cumsum(padded)
    pad_start = pad_end - padded
    dest = jnp.sum(onehot * pad_start[None, :], axis=1) + rank
    n_rows = -(-n_assign // DISPATCH_BLOCK) * DISPATCH_BLOCK + N_EXPERTS * DISPATCH_BLOCK
    n_blocks = n_rows // DISPATCH_BLOCK
    blk_start = jnp.arange(n_blocks, dtype=jnp.int32) * DISPATCH_BLOCK
    blk_e = jnp.minimum(jnp.sum((pad_end[None, :] <= blk_start[:, None]).astype(jnp.int32), axis=1),
                        N_EXPERTS - 1).astype(jnp.int32)
    n_used = (pad_end[-1] // DISPATCH_BLOCK).astype(jnp.int32).reshape(1)
    row_tok = jnp.zeros((n_rows,), jnp.int32).at[dest].set(
        jnp.arange(n_assign, dtype=jnp.int32) // TOP_K_INNER, unique_indices=True)
    return dest.astype(jnp.int32), row_tok, blk_e, n_used


def _layer(x2, b, s, norm_mix_g, w_in, sgu_w, sgu_b, sgu_ln_g, sgu_ln_b, w_proj_attn, w_proj_sgu, w_out,
           norm_ffn_g, w_rg, b_rg, w_re, b_re, w_exp_gate, w_exp_up, w_exp_down):
    t, d = x2.shape
    proj = _inproj(x2, norm_mix_g, w_in.astype(BF16))
    attn = _attention(proj, b, s)

    n_r = N_EXPERT_GROUPS + N_EXPERTS
    w_r = jnp.concatenate([w_rg, w_re, jnp.zeros((d, LANES - n_r), F32)], axis=1)
    wr_hi = w_r.astype(BF16)
    wr_lo = (w_r - wr_hi.astype(F32)).astype(BF16)
    b_r = jnp.concatenate([b_rg, b_re, jnp.zeros((LANES - n_r,), F32)]).reshape(1, LANES)
    x1, h2, rt = _merge(x2, attn, proj, sgu_w, sgu_b, sgu_ln_g, sgu_ln_b,
                        w_proj_attn.astype(BF16), w_proj_sgu.astype(BF16), w_out.astype(BF16),
                        norm_ffn_g, wr_hi, wr_lo, b_r)

    experts = rt[:, 0:TOP_K_INNER].astype(jnp.int32)
    dest, row_tok, blk_e, n_used = _dispatch_plan(experts, t)
    tt = min(ROW_TILE, t)
    dest3 = dest.reshape(t // tt, 1, TOP_K_INNER * tt)
    y = _experts(blk_e, n_used, row_tok, h2, w_exp_gate.astype(BF16), w_exp_up.astype(BF16), w_exp_down.astype(BF16))
    return dest3, x1, rt, y


def kernel(x, norm_mix_g, w_in, sgu_w, sgu_b, sgu_ln_g, sgu_ln_b, w_proj_attn, w_proj_sgu, w_out, norm_ffn_g,
           w_router_group, b_router_group, w_router_expert, b_router_expert, w_exp_gate, w_exp_up, w_exp_down,
           norm_final_g):
    b, s, d = x.shape
    assert w_in.shape[0] == 1, "the combine kernel fuses the closing RMSNorm, so exactly one layer is supported"
    x2 = x.reshape(b * s, d)
    dest3, x1, rt, y = _layer(
        x2, b, s, norm_mix_g[0], w_in[0], sgu_w[0], sgu_b[0], sgu_ln_g[0], sgu_ln_b[0],
        w_proj_attn[0], w_proj_sgu[0], w_out[0], norm_ffn_g[0],
        w_router_group[0], b_router_group[0], w_router_expert[0], b_router_expert[0],
        w_exp_gate[0], w_exp_up[0], w_exp_down[0])
    return _combine(dest3, x1, rt, norm_final_g, y).reshape(b, s, d)
```

```python
import functools

import jax
import jax.numpy as jnp
from jax import lax
from jax.experimental import pallas as pl
from jax.experimental.pallas import tpu as pltpu

F32 = jnp.float32
BF16 = jnp.bfloat16

N_HEADS = 8
HEAD_DIM = 128
D_ATTN = N_HEADS * HEAD_DIM
MOBA_BLOCK = 256
MOBA_TOPK = 3
ROPE_THETA = 500000.0
ROPE_DIM = HEAD_DIM // 4
ROPE_HALF = ROPE_DIM // 2
N_GROUPS_SGU = 8
SGU_GROUP_DIM = 128
D_SGU = N_GROUPS_SGU * SGU_GROUP_DIM
SGU_CHUNK = 128
N_EXPERT_GROUPS = 4
EXPERTS_PER_GROUP = 8
N_EXPERTS = N_EXPERT_GROUPS * EXPERTS_PER_GROUP
TOP_K_INNER = 2
DISPATCH_BLOCK = 256
EPS = 1e-6

LANES = 128
VMEM_LIMIT_BYTES = 56 * 1024 * 1024
LOG2_E = 1.4426950408889634
MASK_BIAS = -1e30

INPROJ_TM = 1024
INPROJ_TN = 1024
MERGE_TM = 256
ROW_TILE = 256


def _dot(a, b):
    return jnp.dot(a, b, preferred_element_type=F32)


def _dot_nt(a, b):
    return lax.dot_general(a, b, (((1,), (1,)), ((), ())), preferred_element_type=F32)


def _sigmoid(x):
    return 1.0 / (1.0 + jnp.exp(-x))


def _gelu_tanh(x):
    return 0.5 * x * (1.0 + jnp.tanh(0.7978845608028654 * (x + 0.044715 * (x * x * x))))


def _inproj_kernel(x_ref, g_ref, w_ref, o_ref, h_ref):
    @pl.when(pl.program_id(1) == 0)
    def _():
        x = x_ref[...]
        ms = jnp.mean(x * x, axis=-1, keepdims=True)
        h_ref[...] = (x * lax.rsqrt(ms + EPS) * g_ref[...]).astype(BF16)

    o_ref[...] = _dot(h_ref[...], w_ref[...]).astype(o_ref.dtype)


def _inproj(x2, g, w_bf16):
    t, d = x2.shape
    n = w_bf16.shape[1]
    tm, tn = min(INPROJ_TM, t), INPROJ_TN
    return pl.pallas_call(
        _inproj_kernel,
        grid=(t // tm, n // tn),
        in_specs=[
            pl.BlockSpec((tm, d), lambda i, j: (i, 0)),
            pl.BlockSpec((1, d), lambda i, j: (0, 0)),
            pl.BlockSpec((d, tn), lambda i, j: (0, j)),
        ],
        out_specs=pl.BlockSpec((tm, tn), lambda i, j: (i, j)),
        out_shape=jax.ShapeDtypeStruct((t, n), BF16),
        scratch_shapes=[pltpu.VMEM((tm, d), BF16)],
        compiler_params=pltpu.CompilerParams(
            dimension_semantics=("parallel", "arbitrary"), vmem_limit_bytes=VMEM_LIMIT_BYTES),
        name="inproj",
    )(x2, g.reshape(1, d), w_bf16)


def _attn_kernel(q_ref, k_ref, v_ref, c_ref, s1_ref, s2_ref, o_ref, kaug_ref, km_ref, s_ref, *, n_blk):
    blk = MOBA_BLOCK
    scale = HEAD_DIM ** -0.5 * LOG2_E
    lane = lax.broadcasted_iota(jnp.int32, (blk, LANES), 1)

    def rope(xb, r0):
        c = c_ref[pl.ds(r0, blk), :]
        s1 = s1_ref[pl.ds(r0, blk), :]
        s2 = s2_ref[pl.ds(r0, blk), :]
        return (xb * c + pltpu.roll(xb, LANES - ROPE_HALF, 1) * s1
                + pltpu.roll(xb, ROPE_HALF, 1) * s2)

    km_ref[...] = jnp.zeros_like(km_ref)

    def kprep(j, carry):
        r0 = pl.multiple_of(j * blk, blk)
        kr = rope(k_ref[pl.ds(r0, blk), :].astype(F32), r0)
        kaug_ref[pl.ds(r0, blk), 0:LANES] = kr.astype(BF16)
        kaug_ref[pl.ds(r0, blk), LANES:2 * LANES] = jnp.where(lane == j, 1.0, 0.0).astype(BF16)
        km_ref[pl.ds(j, 1), :] = jnp.mean(kr, axis=0, keepdims=True)
        return carry

    lax.fori_loop(0, n_blk, kprep, 0)

    row_i = lax.broadcasted_iota(jnp.int32, (blk, blk), 0)
    col_i = lax.broadcasted_iota(jnp.int32, (blk, blk), 1)

    def make_qaug(i):
        r0 = i * blk
        qr = (rope(q_ref[r0:r0 + blk, :].astype(F32), r0) * scale).astype(BF16)
        km = km_ref[...]
        km_hi = km.astype(BF16)
        km_lo = (km - km_hi.astype(F32)).astype(BF16)
        gate = _dot_nt(qr, km_hi) + _dot_nt(qr, km_lo)
        g = jnp.where(lane < i, gate, -jnp.inf)
        sel = lane < 0
        for _ in range(MOBA_TOPK):
            mx = jnp.max(g, axis=-1, keepdims=True)
            idx = jnp.min(jnp.where(g == mx, lane, LANES), axis=-1, keepdims=True)
            hit = lane == idx
            sel = jnp.logical_or(sel, hit)
            g = jnp.where(hit, -jnp.inf, g)
        allowed = jnp.logical_or(jnp.logical_and(sel, lane < i), lane == i)
        bias = jnp.where(allowed, 0.0, MASK_BIAS).astype(BF16)
        return jnp.concatenate([qr, bias], axis=1)

    def score_chunk(i, qaug, c, fold_max):
        s = _dot_nt(qaug, kaug_ref[c * blk:(c + 1) * blk, :])
        if c == i:
            s = jnp.where(col_i <= row_i, s, -jnp.inf)
        s_ref[i % 2, :, c * blk:(c + 1) * blk] = s
        f = jnp.maximum(s[:, 0:LANES], s[:, LANES:2 * LANES])
        return f if fold_max is None else jnp.maximum(fold_max, f)

    def value_chunk(i, m, c, fold_sum, acc):
        p = jnp.exp2(s_ref[i % 2, :, c * blk:(c + 1) * blk] - m)
        f = p[:, 0:LANES] + p[:, LANES:2 * LANES]
        pv = _dot(p.astype(BF16), v_ref[c * blk:(c + 1) * blk, :])
        return (f, pv) if acc is None else (fold_sum + f, acc + pv)

    qaug = make_qaug(0)
    m_cur = jnp.max(score_chunk(0, qaug, 0, None), axis=-1, keepdims=True)
    for i in range(n_blk):
        has_next = i + 1 < n_blk
        n_a = i + 2 if has_next else 0
        n_b = i + 1
        if has_next:
            qaug = make_qaug(i + 1)
        fold_max = fold_sum = acc = None
        for c in range(max(n_a, n_b)):
            if c < n_a:
                fold_max = score_chunk(i + 1, qaug, c, fold_max)
            if c < n_b:
                fold_sum, acc = value_chunk(i, m_cur, c, fold_sum, acc)
        l = jnp.sum(fold_sum, axis=-1, keepdims=True)
        o_ref[i * blk:(i + 1) * blk, :] = (acc / l).astype(o_ref.dtype)
        if has_next:
            m_cur = jnp.max(fold_max, axis=-1, keepdims=True)


def _rope_tables(s):
    inv = ROPE_THETA ** (-jnp.arange(ROPE_HALF, dtype=F32) * 2.0 / ROPE_DIM)
    ang = jnp.arange(s).astype(F32)[:, None] * inv[None, :]
    cos, sin = jnp.cos(ang), jnp.sin(ang)
    zeros = jnp.zeros((s, LANES - ROPE_DIM), F32)
    zh = jnp.zeros((s, ROPE_HALF), F32)
    c = jnp.concatenate([cos, cos, zeros + 1.0], axis=1)
    s1 = jnp.concatenate([-sin, zh, zeros], axis=1)
    s2 = jnp.concatenate([zh, sin, zeros], axis=1)
    return c, s1, s2


def _attention(proj, b, s):
    assert s % MOBA_BLOCK == 0
    n_blk = s // MOBA_BLOCK
    assert n_blk <= LANES
    c, s1, s2 = _rope_tables(s)
    tab = pl.BlockSpec((s, LANES), lambda bi, h: (0, 0))
    return pl.pallas_call(
        functools.partial(_attn_kernel, n_blk=n_blk),
        grid=(b, N_HEADS),
        in_specs=[
            pl.BlockSpec((s, HEAD_DIM), lambda bi, h: (bi, h)),
            pl.BlockSpec((s, HEAD_DIM), lambda bi, h: (bi, N_HEADS + h)),
            pl.BlockSpec((s, HEAD_DIM), lambda bi, h: (bi, 2 * N_HEADS + h)),
            tab, tab, tab,
        ],
        out_specs=pl.BlockSpec((s, HEAD_DIM), lambda bi, h: (bi, h)),
        out_shape=jax.ShapeDtypeStruct((b * s, D_ATTN), BF16),
        scratch_shapes=[pltpu.VMEM((s, 2 * LANES), BF16), pltpu.VMEM((LANES, LANES), F32),
                        pltpu.VMEM((2, MOBA_BLOCK, s), F32)],
        compiler_params=pltpu.CompilerParams(
            dimension_semantics=("parallel", "arbitrary"), vmem_limit_bytes=VMEM_LIMIT_BYTES),
        name="moba_attn",
    )(proj, proj, proj, c, s1, s2)


def _merge_kernel(x_ref, at_ref, u_ref, vg_ref, ga0_ref, ga1_ref, gs0_ref, gs1_ref,
                  sw_ref, sb_ref, lng_ref, lnb_ref, wpa_ref, wpb_ref, wo_ref, g2_ref,
                  wrh_ref, wrl_ref, br_ref,
                  x1_ref, h2_ref, rt_ref,
                  wm_ref, vgn_ref, sgu_ref, mg_ref):
    tm = x_ref.shape[0]
    d = x_ref.shape[1]
    half = d // 2

    @pl.when(pl.program_id(0) == 0)
    def _():
        r = lax.broadcasted_iota(jnp.int32, (SGU_CHUNK, SGU_CHUNK), 0)
        c = lax.broadcasted_iota(jnp.int32, (SGU_CHUNK, SGU_CHUNK), 1)
        for g in range(N_GROUPS_SGU):
            wm_ref[g] = jnp.where(c <= r, sw_ref[g], 0.0).astype(BF16)

    vg = _gelu_tanh(vg_ref[...].astype(F32))
    mu = jnp.mean(vg, axis=-1, keepdims=True)
    var = jnp.mean(jnp.square(vg - mu), axis=-1, keepdims=True)
    vgn_ref[...] = ((vg - mu) * lax.rsqrt(var + EPS) * lng_ref[...] + lnb_ref[...]).astype(BF16)
    for c in range(tm // SGU_CHUNK):
        rows = slice(c * SGU_CHUNK, (c + 1) * SGU_CHUNK)
        for g in range(N_GROUPS_SGU):
            cols = slice(g * SGU_GROUP_DIM, (g + 1) * SGU_GROUP_DIM)
            mixed = _dot(wm_ref[g], vgn_ref[rows, cols]) + sb_ref[:, cols]
            u = _gelu_tanh(u_ref[rows, cols].astype(F32))
            sgu_ref[rows, cols] = (u * mixed).astype(BF16)

    a = _dot(at_ref[...], wpa_ref[...])
    s = _dot(sgu_ref[...], wpb_ref[...])
    mg_ref[:, 0:half] = (_sigmoid(ga0_ref[...].astype(F32)) * a[:, 0:half]
                         + _sigmoid(gs0_ref[...].astype(F32)) * s[:, 0:half]).astype(BF16)
    mg_ref[:, half:d] = (_sigmoid(ga1_ref[...].astype(F32)) * a[:, half:d]
                         + _sigmoid(gs1_ref[...].astype(F32)) * s[:, half:d]).astype(BF16)
    x1 = x_ref[...] + _dot(mg_ref[...], wo_ref[...])
    x1_ref[...] = x1

    h2 = x1 * lax.rsqrt(jnp.mean(x1 * x1, axis=-1, keepdims=True) + EPS) * g2_ref[...]
    h2_ref[...] = h2
    hi = h2.astype(BF16)
    lo = (h2 - hi.astype(F32)).astype(BF16)
    logits = (_dot(hi, wrh_ref[...]) + _dot(lo, wrh_ref[...]) + _dot(hi, wrl_ref[...])) + br_ref[...]

    lane = lax.broadcasted_iota(jnp.int32, (tm, LANES), 1)
    neg = -jnp.inf
    gl = jnp.where(lane < N_EXPERT_GROUPS, logits, neg)
    gmax = jnp.max(gl, axis=-1, keepdims=True)
    grp = jnp.min(jnp.where(gl == gmax, lane, LANES), axis=-1, keepdims=True)
    p_grp = 1.0 / jnp.sum(jnp.exp(gl - gmax), axis=-1, keepdims=True)
    e0 = N_EXPERT_GROUPS + grp * EXPERTS_PER_GROUP
    emask = jnp.logical_and(lane >= e0, lane < e0 + EXPERTS_PER_GROUP)
    el = jnp.where(emask, logits, neg)
    emax = jnp.max(el, axis=-1, keepdims=True)
    ex = jnp.exp(el - emax)
    probs = jnp.where(emask, ex / jnp.sum(ex, axis=-1, keepdims=True), -1.0)
    p1 = jnp.max(probs, axis=-1, keepdims=True)
    i1 = jnp.min(jnp.where(probs == p1, lane, LANES), axis=-1, keepdims=True)
    probs2 = jnp.where(lane == i1, -1.0, probs)
    p2 = jnp.max(probs2, axis=-1, keepdims=True)
    i2 = jnp.min(jnp.where(probs2 == p2, lane, LANES), axis=-1, keepdims=True)
    den = p1 + p2
    w1 = p_grp * (p1 / den)
    w2 = p_grp * (p2 / den)
    ex1 = (i1 - N_EXPERT_GROUPS).astype(F32)
    ex2 = (i2 - N_EXPERT_GROUPS).astype(F32)
    rt_ref[...] = jnp.where(lane == 0, ex1, jnp.where(lane == 1, ex2,
                            jnp.where(lane == 2, w1, jnp.where(lane == 3, w2, 0.0))))


def _merge(x2, attn, proj, sgu_w, sgu_b, ln_g, ln_b, wpa, wpb, wo, g2, wr_hi, wr_lo, br):
    t, d = x2.shape
    tm = MERGE_TM
    assert d == 2 * D_ATTN and t % tm == 0
    u_blk = 3 * D_ATTN // D_SGU
    sb_full = jnp.repeat(sgu_b.T, SGU_GROUP_DIM, axis=1)

    def col(j):
        return pl.BlockSpec((tm, D_SGU), lambda i: (i, j))

    def const(shape):
        return pl.BlockSpec(shape, lambda i: tuple(0 for _ in shape))

    return pl.pallas_call(
        _merge_kernel,
        grid=(t // tm,),
        in_specs=[
            pl.BlockSpec((tm, d), lambda i: (i, 0)),
            pl.BlockSpec((tm, D_ATTN), lambda i: (i, 0)),
            col(u_blk), col(u_blk + 1), col(u_blk + 2), col(u_blk + 3), col(u_blk + 4), col(u_blk + 5),
            const((N_GROUPS_SGU, SGU_CHUNK, SGU_CHUNK)),
            const((SGU_CHUNK, D_SGU)),
            const((1, D_SGU)), const((1, D_SGU)),
            const((D_ATTN, d)), const((D_SGU, d)), const((d, d)),
            const((1, d)),
            const((d, LANES)), const((d, LANES)), const((1, LANES)),
        ],
        out_specs=[
            pl.BlockSpec((tm, d), lambda i: (i, 0)),
            pl.BlockSpec((tm, d), lambda i: (i, 0)),
            pl.BlockSpec((tm, LANES), lambda i: (i, 0)),
        ],
        out_shape=[
            jax.ShapeDtypeStruct((t, d), F32),
            jax.ShapeDtypeStruct((t, d), F32),
            jax.ShapeDtypeStruct((t, LANES), F32),
        ],
        scratch_shapes=[
            pltpu.VMEM((N_GROUPS_SGU, SGU_CHUNK, SGU_CHUNK), BF16),
            pltpu.VMEM((tm, D_SGU), BF16),
            pltpu.VMEM((tm, D_SGU), BF16),
            pltpu.VMEM((tm, d), BF16),
        ],
        compiler_params=pltpu.CompilerParams(
            dimension_semantics=("arbitrary",), vmem_limit_bytes=VMEM_LIMIT_BYTES),
        name="merge_router",
    )(x2, attn, proj, proj, proj, proj, proj, proj, sgu_w, sb_full,
      ln_g.reshape(1, -1), ln_b.reshape(1, -1), wpa, wpb, wo, g2.reshape(1, -1), wr_hi, wr_lo, br)


def _row_copy(src, si, dst, di, sem):
    return pltpu.make_async_copy(src.at[pl.ds(si, 1)], dst.at[pl.ds(di, 1)], sem)


def _gather_block_rows(tok_ref, src_hbm, dst, sem):
    def issue(r, carry):
        _row_copy(src_hbm, tok_ref[0, 0, r], dst, r, sem).start()
        return carry

    lax.fori_loop(0, DISPATCH_BLOCK, issue, 0, unroll=8)


def _wait_block_rows(src_hbm, dst, sem):
    def drain(r, carry):
        _row_copy(src_hbm, 0, dst, 0, sem).wait()
        return carry

    lax.fori_loop(0, DISPATCH_BLOCK, drain, 0, unroll=8)


def _expert_kernel(be_ref, nu_ref, tok_ref, tok_next_ref, h2_hbm, wg_ref, wu_ref, wd_ref, y_ref, xbuf, sems):
    del be_ref
    i = pl.program_id(0)
    n_used = nu_ref[0]
    slot = i % 2

    @pl.when(jnp.logical_and(i == 0, n_used > 0))
    def _():
        _gather_block_rows(tok_ref, h2_hbm, xbuf.at[0], sems.at[0])

    @pl.when(i + 1 < n_used)
    def _():
        _gather_block_rows(tok_next_ref, h2_hbm, xbuf.at[1 - slot], sems.at[1 - slot])

    @pl.when(i < n_used)
    def _():
        _wait_block_rows(h2_hbm, xbuf.at[slot], sems.at[slot])
        h = xbuf[slot].astype(BF16)
        g = _dot(h, wg_ref[0])
        u = _dot(h, wu_ref[0])
        hid = (g * _sigmoid(g) * u).astype(BF16)
        y_ref[...] = _dot(hid, wd_ref[0])

    @pl.when(i >= n_used)
    def _():
        y_ref[...] = jnp.zeros_like(y_ref)


def _experts(blk_e, n_used, row_tok, h2, wg, wu, wd):
    d = h2.shape[1]
    ff = wg.shape[2]
    n_blocks = row_tok.shape[0] // DISPATCH_BLOCK
    tok3 = row_tok.reshape(n_blocks, 1, DISPATCH_BLOCK)
    grid_spec = pltpu.PrefetchScalarGridSpec(
        num_scalar_prefetch=2,
        grid=(n_blocks,),
        in_specs=[
            pl.BlockSpec((1, 1, DISPATCH_BLOCK), lambda i, be, nu: (i, 0, 0), memory_space=pltpu.SMEM),
            pl.BlockSpec((1, 1, DISPATCH_BLOCK), lambda i, be, nu: (jnp.minimum(i + 1, n_blocks - 1), 0, 0),
                         memory_space=pltpu.SMEM),
            pl.BlockSpec(memory_space=pl.ANY),
            pl.BlockSpec((1, d, ff), lambda i, be, nu: (be[i], 0, 0)),
            pl.BlockSpec((1, d, ff), lambda i, be, nu: (be[i], 0, 0)),
            pl.BlockSpec((1, ff, d), lambda i, be, nu: (be[i], 0, 0)),
        ],
        out_specs=pl.BlockSpec((DISPATCH_BLOCK, d), lambda i, be, nu: (i, 0)),
        scratch_shapes=[pltpu.VMEM((2, DISPATCH_BLOCK, d), F32), pltpu.SemaphoreType.DMA((2,))],
    )
    return pl.pallas_call(
        _expert_kernel,
        grid_spec=grid_spec,
        out_shape=jax.ShapeDtypeStruct((n_blocks * DISPATCH_BLOCK, d), F32),
        compiler_params=pltpu.CompilerParams(
            dimension_semantics=("arbitrary",), vmem_limit_bytes=VMEM_LIMIT_BYTES),
        name="moe_experts",
    )(blk_e, n_used, tok3, tok3, h2, wg, wu, wd)


def _combine_kernel(dest_ref, dest_next_ref, x1_ref, rt_ref, gf_ref, y_hbm, o_ref, ybuf, sems):
    tt = x1_ref.shape[0]
    i = pl.program_id(0)
    slot = i % 2

    def gather(d_ref, s):
        def issue(t, carry):
            for k in range(TOP_K_INNER):
                _row_copy(y_hbm, d_ref[0, 0, TOP_K_INNER * t + k], ybuf.at[s].at[k], t, sems.at[s]).start()
            return carry

        lax.fori_loop(0, tt, issue, 0, unroll=4)

    @pl.when(i == 0)
    def _():
        gather(dest_ref, 0)

    @pl.when(i + 1 < pl.num_programs(0))
    def _():
        gather(dest_next_ref, 1 - slot)

    def drain(t, carry):
        for k in range(TOP_K_INNER):
            _row_copy(y_hbm, 0, ybuf.at[slot].at[k], 0, sems.at[slot]).wait()
        return carry

    lax.fori_loop(0, tt, drain, 0, unroll=4)

    rt = rt_ref[...]
    w1 = rt[:, 2:3]
    w2 = rt[:, 3:4]
    z = x1_ref[...] + (w1 * ybuf[slot, 0] + w2 * ybuf[slot, 1])
    o_ref[...] = z * lax.rsqrt(jnp.mean(z * z, axis=-1, keepdims=True) + EPS) * gf_ref[...]


def _combine(dest3, x1, rt, gf, y):
    t, d = x1.shape
    tt = dest3.shape[2] // TOP_K_INNER
    n_tiles = t // tt
    return pl.pallas_call(
        _combine_kernel,
        grid=(n_tiles,),
        in_specs=[
            pl.BlockSpec((1, 1, TOP_K_INNER * tt), lambda i: (i, 0, 0), memory_space=pltpu.SMEM),
            pl.BlockSpec((1, 1, TOP_K_INNER * tt), lambda i: (jnp.minimum(i + 1, n_tiles - 1), 0, 0),
                         memory_space=pltpu.SMEM),
            pl.BlockSpec((tt, d), lambda i: (i, 0)),
            pl.BlockSpec((tt, LANES), lambda i: (i, 0)),
            pl.BlockSpec((1, d), lambda i: (0, 0)),
            pl.BlockSpec(memory_space=pl.ANY),
        ],
        out_specs=pl.BlockSpec((tt, d), lambda i: (i, 0)),
        out_shape=jax.ShapeDtypeStruct((t, d), F32),
        scratch_shapes=[pltpu.VMEM((2, TOP_K_INNER, tt, d), F32), pltpu.SemaphoreType.DMA((2,))],
        compiler_params=pltpu.CompilerParams(
            dimension_semantics=("arbitrary",), vmem_limit_bytes=VMEM_LIMIT_BYTES),
        name="moe_combine",
    )(dest3, dest3, x1, rt, gf.reshape(1, d), y)


def _dispatch_plan(experts, t):
    n_assign = t * TOP_K_INNER
    flat_e = experts.reshape(n_assign)
    onehot = (flat_e[:, None] == jnp.arange(N_EXPERTS, dtype=jnp.int32)[None, :]).astype(jnp.int32)
    csum = jnp.cumsum(onehot, axis=0)
    counts = csum[-1]
    rank = jnp.sum(onehot * csum, axis=1) - 1
    padded = (counts + DISPATCH_BLOCK - 1) // DISPATCH_BLOCK * DISPATCH_BLOCK
    pad_end = jnp.cumsum(padded)
    pad_start = pad_end - padded
    dest = jnp.sum(onehot * pad_start[None, :], axis=1) + rank
    n_rows = -(-n_assign // DISPATCH_BLOCK) * DISPATCH_BLOCK + N_EXPERTS * DISPATCH_BLOCK
    n_blocks = n_rows // DISPATCH_BLOCK
    blk_start = jnp.arange(n_blocks, dtype=jnp.int32) * DISPATCH_BLOCK
    blk_e = jnp.minimum(jnp.sum((pad_end[None, :] <= blk_start[:, None]).astype(jnp.int32), axis=1),
                        N_EXPERTS - 1).astype(jnp.int32)
    n_used = (pad_end[-1] // DISPATCH_BLOCK).astype(jnp.int32).reshape(1)
    row_tok = jnp.zeros((n_rows,), jnp.int32).at[dest].set(
        jnp.arange(n_assign, dtype=jnp.int32) // TOP_K_INNER, unique_indices=True)
    return dest.astype(jnp.int32), row_tok, blk_e, n_used


def _layer(x2, b, s, norm_mix_g, w_in, sgu_w, sgu_b, sgu_ln_g, sgu_ln_b, w_proj_attn, w_proj_sgu, w_out,
           norm_ffn_g, w_rg, b_rg, w_re, b_re, w_exp_gate, w_exp_up, w_exp_down):
    t, d = x2.shape
    proj = _inproj(x2, norm_mix_g, w_in.astype(BF16))
    attn = _attention(proj, b, s)

    n_r = N_EXPERT_GROUPS + N_EXPERTS
    w_r = jnp.concatenate([w_rg, w_re, jnp.zeros((d, LANES - n_r), F32)], axis=1)
    wr_hi = w_r.astype(BF16)
    wr_lo = (w_r - wr_hi.astype(F32)).astype(BF16)
    b_r = jnp.concatenate([b_rg, b_re, jnp.zeros((LANES - n_r,), F32)]).reshape(1, LANES)
    x1, h2, rt = _merge(x2, attn, proj, sgu_w, sgu_b, sgu_ln_g, sgu_ln_b,
                        w_proj_attn.astype(BF16), w_proj_sgu.astype(BF16), w_out.astype(BF16),
                        norm_ffn_g, wr_hi, wr_lo, b_r)

    experts = rt[:, 0:TOP_K_INNER].astype(jnp.int32)
    dest, row_tok, blk_e, n_used = _dispatch_plan(experts, t)
    tt = min(ROW_TILE, t)
    dest3 = dest.reshape(t // tt, 1, TOP_K_INNER * tt)
    y = _experts(blk_e, n_used, row_tok, h2, w_exp_gate.astype(BF16), w_exp_up.astype(BF16), w_exp_down.astype(BF16))
    return dest3, x1, rt, y


def kernel(x, norm_mix_g, w_in, sgu_w, sgu_b, sgu_ln_g, sgu_ln_b, w_proj_attn, w_proj_sgu, w_out, norm_ffn_g,
           w_router_group, b_router_group, w_router_expert, b_router_expert, w_exp_gate, w_exp_up, w_exp_down,
           norm_final_g):
    b, s, d = x.shape
    assert w_in.shape[0] == 1, "the combine kernel fuses the closing RMSNorm, so exactly one layer is supported"
    x2 = x.reshape(b * s, d)
    dest3, x1, rt, y = _layer(
        x2, b, s, norm_mix_g[0], w_in[0], sgu_w[0], sgu_b[0], sgu_ln_g[0], sgu_ln_b[0],
        w_proj_attn[0], w_proj_sgu[0], w_out[0], norm_ffn_g[0],
        w_router_group[0], b_router_group[0], w_router_expert[0], b_router_expert[0],
        w_exp_gate[0], w_exp_up[0], w_exp_down[0])
    return _combine(dest3, x1, rt, norm_final_g, y).reshape(b, s, d)
```

```python
import functools

import jax
import jax.numpy as jnp
from jax import lax
from jax.experimental import pallas as pl
from jax.experimental.pallas import tpu as pltpu

F32 = jnp.float32
BF16 = jnp.bfloat16

N_HEADS = 8
HEAD_DIM = 128
D_ATTN = N_HEADS * HEAD_DIM
MOBA_BLOCK = 256
MOBA_TOPK = 3
ROPE_THETA = 500000.0
ROPE_DIM = HEAD_DIM // 4
ROPE_HALF = ROPE_DIM // 2
N_GROUPS_SGU = 8
SGU_GROUP_DIM = 128
D_SGU = N_GROUPS_SGU * SGU_GROUP_DIM
SGU_CHUNK = 128
N_EXPERT_GROUPS = 4
EXPERTS_PER_GROUP = 8
N_EXPERTS = N_EXPERT_GROUPS * EXPERTS_PER_GROUP
TOP_K_INNER = 2
DISPATCH_BLOCK = 256
EPS = 1e-6

LANES = 128
VMEM_LIMIT_BYTES = 56 * 1024 * 1024
LOG2_E = 1.4426950408889634
MASK_BIAS = -1e30

INPROJ_TM = 1024
INPROJ_TN = 1024
MERGE_TM = 256
ROW_TILE = 256


def _dot(a, b):
    return jnp.dot(a, b, preferred_element_type=F32)


def _dot_nt(a, b):
    return lax.dot_general(a, b, (((1,), (1,)), ((), ())), preferred_element_type=F32)


def _sigmoid(x):
    return 1.0 / (1.0 + jnp.exp(-x))


def _gelu_tanh(x):
    return 0.5 * x * (1.0 + jnp.tanh(0.7978845608028654 * (x + 0.044715 * (x * x * x))))


def _inproj_kernel(x_ref, g_ref, w_ref, o_ref, h_ref):
    @pl.when(pl.program_id(1) == 0)
    def _():
        x = x_ref[...]
        ms = jnp.mean(x * x, axis=-1, keepdims=True)
        h_ref[...] = (x * lax.rsqrt(ms + EPS) * g_ref[...]).astype(BF16)

    o_ref[...] = _dot(h_ref[...], w_ref[...]).astype(o_ref.dtype)


def _inproj(x2, g, w_bf16):
    t, d = x2.shape
    n = w_bf16.shape[1]
    tm, tn = min(INPROJ_TM, t), INPROJ_TN
    return pl.pallas_call(
        _inproj_kernel,
        grid=(t // tm, n // tn),
        in_specs=[
            pl.BlockSpec((tm, d), lambda i, j: (i, 0)),
            pl.BlockSpec((1, d), lambda i, j: (0, 0)),
            pl.BlockSpec((d, tn), lambda i, j: (0, j)),
        ],
        out_specs=pl.BlockSpec((tm, tn), lambda i, j: (i, j)),
        out_shape=jax.ShapeDtypeStruct((t, n), BF16),
        scratch_shapes=[pltpu.VMEM((tm, d), BF16)],
        compiler_params=pltpu.CompilerParams(
            dimension_semantics=("parallel", "arbitrary"), vmem_limit_bytes=VMEM_LIMIT_BYTES),
        name="inproj",
    )(x2, g.reshape(1, d), w_bf16)


def _attn_kernel(q_ref, k_ref, v_ref, c_ref, s1_ref, s2_ref, o_ref, kaug_ref, km_ref, s_ref, *, n_blk):
    blk = MOBA_BLOCK
    scale = HEAD_DIM ** -0.5 * LOG2_E
    lane = lax.broadcasted_iota(jnp.int32, (blk, LANES), 1)

    def rope(xb, r0):
        c = c_ref[pl.ds(r0, blk), :]
        s1 = s1_ref[pl.ds(r0, blk), :]
        s2 = s2_ref[pl.ds(r0, blk), :]
        return (xb * c + pltpu.roll(xb, LANES - ROPE_HALF, 1) * s1
                + pltpu.roll(xb, ROPE_HALF, 1) * s2)

    km_ref[...] = jnp.zeros_like(km_ref)

    def kprep(j, carry):
        r0 = pl.multiple_of(j * blk, blk)
        kr = rope(k_ref[pl.ds(r0, blk), :].astype(F32), r0)
        kaug_ref[pl.ds(r0, blk), 0:LANES] = kr.astype(BF16)
        kaug_ref[pl.ds(r0, blk), LANES:2 * LANES] = jnp.where(lane == j, 1.0, 0.0).astype(BF16)
        km_ref[pl.ds(j, 1), :] = jnp.mean(kr, axis=0, keepdims=True)
        return carry

    lax.fori_loop(0, n_blk, kprep, 0)

    row_i = lax.broadcasted_iota(jnp.int32, (blk, blk), 0)
    col_i = lax.broadcasted_iota(jnp.int32, (blk, blk), 1)

    def make_qaug(i):
        r0 = i * blk
        qr = (rope(q_ref[r0:r0 + blk, :].astype(F32), r0) * scale).astype(BF16)
        km = km_ref[...]
        km_hi = km.astype(BF16)
        km_lo = (km - km_hi.astype(F32)).astype(BF16)
        gate = _dot_nt(qr, km_hi) + _dot_nt(qr, km_lo)
        g = jnp.where(lane < i, gate, -jnp.inf)
        sel = lane < 0
        for _ in range(MOBA_TOPK):
            mx = jnp.max(g, axis=-1, keepdims=True)
            idx = jnp.min(jnp.where(g == mx, lane, LANES), axis=-1, keepdims=True)
            hit = lane == idx
            sel = jnp.logical_or(sel, hit)
            g = jnp.where(hit, -jnp.inf, g)
        allowed = jnp.logical_or(jnp.logical_and(sel, lane < i), lane == i)
        bias = jnp.where(allowed, 0.0, MASK_BIAS).astype(BF16)
        return jnp.concatenate([qr, bias], axis=1)

    def score_chunk(i, qaug, c, fold_max):
        s = _dot_nt(qaug, kaug_ref[c * blk:(c + 1) * blk, :])
        if c == i:
            s = jnp.where(col_i <= row_i, s, -jnp.inf)
        s_ref[i % 2, :, c * blk:(c + 1) * blk] = s
        f = jnp.maximum(s[:, 0:LANES], s[:, LANES:2 * LANES])
        return f if fold_max is None else jnp.maximum(fold_max, f)

    def value_chunk(i, m, c, fold_sum, acc):
        p = jnp.exp2(s_ref[i % 2, :, c * blk:(c + 1) * blk] - m)
        f = p[:, 0:LANES] + p[:, LANES:2 * LANES]
        pv = _dot(p.astype(BF16), v_ref[c * blk:(c + 1) * blk, :])
        return (f, pv) if acc is None else (fold_sum + f, acc + pv)

    qaug = make_qaug(0)
    m_cur = jnp.max(score_chunk(0, qaug, 0, None), axis=-1, keepdims=True)
    for i in range(n_blk):
        has_next = i + 1 < n_blk
        n_a = i + 2 if has_next else 0
        n_b = i + 1
        if has_next:
            qaug = make_qaug(i + 1)
        fold_max = fold_sum = acc = None
        for c in range(max(n_a, n_b)):
            if c < n_a:
                fold_max = score_chunk(i + 1, qaug, c, fold_max)
            if c < n_b:
                fold_sum, acc = value_chunk(i, m_cur, c, fold_sum, acc)
        l = jnp.sum(fold_sum, axis=-1, keepdims=True)
        o_ref[i * blk:(i + 1) * blk, :] = (acc / l).astype(o_ref.dtype)
        if has_next:
            m_cur = jnp.max(fold_max, axis=-1, keepdims=True)


def _rope_tables(s):
    inv = ROPE_THETA ** (-jnp.arange(ROPE_HALF, dtype=F32) * 2.0 / ROPE_DIM)
    ang = jnp.arange(s).astype(F32)[:, None] * inv[None, :]
    cos, sin = jnp.cos(ang), jnp.sin(ang)
    zeros = jnp.zeros((s, LANES - ROPE_DIM), F32)
    zh = jnp.zeros((s, ROPE_HALF), F32)
    c = jnp.concatenate([cos, cos, zeros + 1.0], axis=1)
    s1 = jnp.concatenate([-sin, zh, zeros], axis=1)
    s2 = jnp.concatenate([zh, sin, zeros], axis=1)
    return c, s1, s2


def _attention(proj, b, s):
    assert s % MOBA_BLOCK == 0
    n_blk = s // MOBA_BLOCK
    assert n_blk <= LANES
    c, s1, s2 = _rope_tables(s)
    tab = pl.BlockSpec((s, LANES), lambda bi, h: (0, 0))
    return pl.pallas_call(
        functools.partial(_attn_kernel, n_blk=n_blk),
        grid=(b, N_HEADS),
        in_specs=[
            pl.BlockSpec((s, HEAD_DIM), lambda bi, h: (bi, h)),
            pl.BlockSpec((s, HEAD_DIM), lambda bi, h: (bi, N_HEADS + h)),
            pl.BlockSpec((s, HEAD_DIM), lambda bi, h: (bi, 2 * N_HEADS + h)),
            tab, tab, tab,
        ],
        out_specs=pl.BlockSpec((s, HEAD_DIM), lambda bi, h: (bi, h)),
        out_shape=jax.ShapeDtypeStruct((b * s, D_ATTN), BF16),
        scratch_shapes=[pltpu.VMEM((s, 2 * LANES), BF16), pltpu.VMEM((LANES, LANES), F32),
                        pltpu.VMEM((2, MOBA_BLOCK, s), F32)],
        compiler_params=pltpu.CompilerParams(
            dimension_semantics=("parallel", "arbitrary"), vmem_limit_bytes=VMEM_LIMIT_BYTES),
        name="moba_attn",
    )(proj, proj, proj, c, s1, s2)


def _merge_kernel(x_ref, at_ref, u_ref, vg_ref, ga0_ref, ga1_ref, gs0_ref, gs1_ref,
                  sw_ref, sb_ref, lng_ref, lnb_ref, wpa_ref, wpb_ref, wo_ref, g2_ref,
                  wrh_ref, wrl_ref, br_ref,
                  x1_ref, h2_ref, rt_ref,
                  wm_ref, vgn_ref, sgu_ref, mg_ref):
    tm = x_ref.shape[0]
    d = x_ref.shape[1]
    half = d // 2

    @pl.when(pl.program_id(0) == 0)
    def _():
        r = lax.broadcasted_iota(jnp.int32, (SGU_CHUNK, SGU_CHUNK), 0)
        c = lax.broadcasted_iota(jnp.int32, (SGU_CHUNK, SGU_CHUNK), 1)
        for g in range(N_GROUPS_SGU):
            wm_ref[g] = jnp.where(c <= r, sw_ref[g], 0.0).astype(BF16)

    vg = _gelu_tanh(vg_ref[...].astype(F32))
    mu = jnp.mean(vg, axis=-1, keepdims=True)
    var = jnp.mean(jnp.square(vg - mu), axis=-1, keepdims=True)
    vgn_ref[...] = ((vg - mu) * lax.rsqrt(var + EPS) * lng_ref[...] + lnb_ref[...]).astype(BF16)
    for c in range(tm // SGU_CHUNK):
        rows = slice(c * SGU_CHUNK, (c + 1) * SGU_CHUNK)
        for g in range(N_GROUPS_SGU):
            cols = slice(g * SGU_GROUP_DIM, (g + 1) * SGU_GROUP_DIM)
            mixed = _dot(wm_ref[g], vgn_ref[rows, cols]) + sb_ref[:, cols]
            u = _gelu_tanh(u_ref[rows, cols].astype(F32))
            sgu_ref[rows, cols] = (u * mixed).astype(BF16)

    a = _dot(at_ref[...], wpa_ref[...])
    s = _dot(sgu_ref[...], wpb_ref[...])
    mg_ref[:, 0:half] = (_sigmoid(ga0_ref[...].astype(F32)) * a[:, 0:half]
                         + _sigmoid(gs0_ref[...].astype(F32)) * s[:, 0:half]).astype(BF16)
    mg_ref[:, half:d] = (_sigmoid(ga1_ref[...].astype(F32)) * a[:, half:d]
                         + _sigmoid(gs1_ref[...].astype(F32)) * s[:, half:d]).astype(BF16)
    x1 = x_ref[...] + _dot(mg_ref[...], wo_ref[...])
    x1_ref[...] = x1

    h2 = x1 * lax.rsqrt(jnp.mean(x1 * x1, axis=-1, keepdims=True) + EPS) * g2_ref[...]
    h2_ref[...] = h2
    hi = h2.astype(BF16)
    lo = (h2 - hi.astype(F32)).astype(BF16)
    logits = (_dot(hi, wrh_ref[...]) + _dot(lo, wrh_ref[...]) + _dot(hi, wrl_ref[...])) + br_ref[...]
    lane = lax.broadcasted_iota(jnp.int32, (tm, LANES), 1)
    neg = -jnp.inf
    gl = jnp.where(lane < N_EXPERT_GROUPS, logits, neg)
    gmax = jnp.max(gl, axis=-1, keepdims=True)
    grp = jnp.min(jnp.where(gl == gmax, lane, LANES), axis=-1, keepdims=True)
    p_grp = 1.0 / jnp.sum(jnp.exp(gl - gmax), axis=-1, keepdims=True)
    e0 = N_EXPERT_GROUPS + grp * EXPERTS_PER_GROUP
    emask = jnp.logical_and(lane >= e0, lane < e0 + EXPERTS_PER_GROUP)
    el = jnp.where(emask, logits, neg)
    emax = jnp.max(el, axis=-1, keepdims=True)
    ex = jnp.exp(el - emax)
    probs = jnp.where(emask, ex / jnp.sum(ex, axis=-1, keepdims=True), -1.0)
    p1 = jnp.max(probs, axis=-1, keepdims=True)
    i1 = jnp.min(jnp.where(probs == p1, lane, LANES), axis=-1, keepdims=True)
    probs2 = jnp.where(lane == i1, -1.0, probs)
    p2 = jnp.max(probs2, axis=-1, keepdims=True)
    i2 = jnp.min(jnp.where(probs2 == p2, lane, LANES), axis=-1, keepdims=True)
    den = p1 + p2
    w1 = p_grp * (p1 / den)
    w2 = p_grp * (p2 / den)
    ex1 = (i1 - N_EXPERT_GROUPS).astype(F32)
    ex2 = (i2 - N_EXPERT_GROUPS).astype(F32)
    rt_ref[...] = jnp.where(lane == 0, ex1, jnp.where(lane == 1, ex2,
                            jnp.where(lane == 2, w1, jnp.where(lane == 3, w2, 0.0))))


def _merge(x2, attn, proj, sgu_w, sgu_b, ln_g, ln_b, wpa, wpb, wo, g2, wr_hi, wr_lo, br):
    t, d = x2.shape
    tm = MERGE_TM
    assert d == 2 * D_ATTN and t % tm == 0
    u_blk = 3 * D_ATTN // D_SGU
    sb_full = jnp.repeat(sgu_b.T, SGU_GROUP_DIM, axis=1)

    def col(j):
        return pl.BlockSpec((tm, D_SGU), lambda i: (i, j))

    def const(shape):
        return pl.BlockSpec(shape, lambda i: tuple(0 for _ in shape))

    return pl.pallas_call(
        _merge_kernel,
        grid=(t // tm,),
        in_specs=[
            pl.BlockSpec((tm, d), lambda i: (i, 0)),
            pl.BlockSpec((tm, D_ATTN), lambda i: (i, 0)),
            col(u_blk), col(u_blk + 1), col(u_blk + 2), col(u_blk + 3), col(u_blk + 4), col(u_blk + 5),
            const((N_GROUPS_SGU, SGU_CHUNK, SGU_CHUNK)),
            const((SGU_CHUNK, D_SGU)),
            const((1, D_SGU)), const((1, D_SGU)),
            const((D_ATTN, d)), const((D_SGU, d)), const((d, d)),
            const((1, d)),
            const((d, LANES)), const((d, LANES)), const((1, LANES)),
        ],
        out_specs=[
            pl.BlockSpec((tm, d), lambda i: (i, 0)),
            pl.BlockSpec((tm, d), lambda i: (i, 0)),
            pl.BlockSpec((tm, LANES), lambda i: (i, 0)),
        ],
        out_shape=[
            jax.ShapeDtypeStruct((t, d), F32),
            jax.ShapeDtypeStruct((t, d), F32),
            jax.ShapeDtypeStruct((t, LANES), F32),
        ],
        scratch_shapes=[
            pltpu.VMEM((N_GROUPS_SGU, SGU_CHUNK, SGU_CHUNK), BF16),
            pltpu.VMEM((tm, D_SGU), BF16),
            pltpu.VMEM((tm, D_SGU), BF16),
            pltpu.VMEM((tm, d), BF16),
        ],
        compiler_params=pltpu.CompilerParams(
            dimension_semantics=("arbitrary",), vmem_limit_bytes=VMEM_LIMIT_BYTES),
        name="merge_router",
    )(x2, attn, proj, proj, proj, proj, proj, proj, sgu_w, sb_full,
      ln_g.reshape(1, -1), ln_b.reshape(1, -1), wpa, wpb, wo, g2.reshape(1, -1), wr_hi, wr_lo, br)


def _row_copy(src, si, dst, di, sem):
    return pltpu.make_async_copy(src.at[pl.ds(si, 1)], dst.at[pl.ds(di, 1)], sem)


def _gather_block_rows(tok_ref, src_hbm, dst, sem):
    for r in range(DISPATCH_BLOCK):
        _row_copy(src_hbm, tok_ref[0, 0, r], dst, r, sem).start()


def _wait_block_rows(src_hbm, dst, sem):
    def drain(r, carry):
        _row_copy(src_hbm, 0, dst, 0, sem).wait()
        return carry

    lax.fori_loop(0, DISPATCH_BLOCK, drain, 0, unroll=8)


def _expert_kernel(be_ref, nu_ref, tok_ref, tok_next_ref, h2_hbm, wg_ref, wu_ref, wd_ref, y_ref, xbuf, sems):
    del be_ref
    i = pl.program_id(0)
    n_used = nu_ref[0]
    slot = i % 2

    @pl.when(jnp.logical_and(i == 0, n_used > 0))
    def _():
        _gather_block_rows(tok_ref, h2_hbm, xbuf.at[0], sems.at[0])

    @pl.when(i + 1 < n_used)
    def _():
        _gather_block_rows(tok_next_ref, h2_hbm, xbuf.at[1 - slot], sems.at[1 - slot])

    @pl.when(i < n_used)
    def _():
        _wait_block_rows(h2_hbm, xbuf.at[slot], sems.at[slot])
        h = xbuf[slot].astype(BF16)
        g = _dot(h, wg_ref[0])
        u = _dot(h, wu_ref[0])
        hid = (g * _sigmoid(g) * u).astype(BF16)
        y_ref[...] = _dot(hid, wd_ref[0])

    @pl.when(i >= n_used)
    def _():
        y_ref[...] = jnp.zeros_like(y_ref)


def _experts(blk_e, n_used, row_tok, h2, wg, wu, wd):
    d = h2.shape[1]
    ff = wg.shape[2]
    n_blocks = row_tok.shape[0] // DISPATCH_BLOCK
    tok3 = row_tok.reshape(n_blocks, 1, DISPATCH_BLOCK)
    grid_spec = pltpu.PrefetchScalarGridSpec(
        num_scalar_prefetch=2,
        grid=(n_blocks,),
        in_specs=[
            pl.BlockSpec((1, 1, DISPATCH_BLOCK), lambda i, be, nu: (i, 0, 0), memory_space=pltpu.SMEM),
            pl.BlockSpec((1, 1, DISPATCH_BLOCK), lambda i, be, nu: (jnp.minimum(i + 1, n_blocks - 1), 0, 0),
                         memory_space=pltpu.SMEM),
            pl.BlockSpec(memory_space=pl.ANY),
            pl.BlockSpec((1, d, ff), lambda i, be, nu: (be[i], 0, 0)),
            pl.BlockSpec((1, d, ff), lambda i, be, nu: (be[i], 0, 0)),
            pl.BlockSpec((1, ff, d), lambda i, be, nu: (be[i], 0, 0)),
        ],
        out_specs=pl.BlockSpec((DISPATCH_BLOCK, d), lambda i, be, nu: (i, 0)),
        scratch_shapes=[pltpu.VMEM((2, DISPATCH_BLOCK, d), F32), pltpu.SemaphoreType.DMA((2,))],
    )
    return pl.pallas_call(
        _expert_kernel,
        grid_spec=grid_spec,
        out_shape=jax.ShapeDtypeStruct((n_blocks * DISPATCH_BLOCK, d), F32),
        compiler_params=pltpu.CompilerParams(
            dimension_semantics=("arbitrary",), vmem_limit_bytes=VMEM_LIMIT_BYTES),
        name="moe_experts",
    )(blk_e, n_used, tok3, tok3, h2, wg, wu, wd)


def _combine_kernel(dest_ref, dest_next_ref, x1_ref, rt_ref, gf_ref, y_hbm, o_ref, ybuf, sems):
    tt = x1_ref.shape[0]
    i = pl.program_id(0)
    slot = i % 2

    def gather(d_ref, s):
        for t in range(tt):
            for k in range(TOP_K_INNER):
                _row_copy(y_hbm, d_ref[0, 0, TOP_K_INNER * t + k], ybuf.at[s].at[k], t, sems.at[s]).start()

    @pl.when(i == 0)
    def _():
        gather(dest_ref, 0)

    @pl.when(i + 1 < pl.num_programs(0))
    def _():
        gather(dest_next_ref, 1 - slot)

    def drain(t, carry):
        for k in range(TOP_K_INNER):
            _row_copy(y_hbm, 0, ybuf.at[slot].at[k], 0, sems.at[slot]).wait()
        return carry

    lax.fori_loop(0, tt, drain, 0, unroll=4)

    rt = rt_ref[...]
    w1 = rt[:, 2:3]
    w2 = rt[:, 3:4]
    z = x1_ref[...] + (w1 * ybuf[slot, 0] + w2 * ybuf[slot, 1])
    o_ref[...] = z * lax.rsqrt(jnp.mean(z * z, axis=-1, keepdims=True) + EPS) * gf_ref[...]


def _combine(dest3, x1, rt, gf, y):
    t, d = x1.shape
    tt = dest3.shape[2] // TOP_K_INNER
    n_tiles = t // tt
    return pl.pallas_call(
        _combine_kernel,
        grid=(n_tiles,),
        in_specs=[
            pl.BlockSpec((1, 1, TOP_K_INNER * tt), lambda i: (i, 0, 0), memory_space=pltpu.SMEM),
            pl.BlockSpec((1, 1, TOP_K_INNER * tt), lambda i: (jnp.minimum(i + 1, n_tiles - 1), 0, 0),
                         memory_space=pltpu.SMEM),
            pl.BlockSpec((tt, d), lambda i: (i, 0)),
            pl.BlockSpec((tt, LANES), lambda i: (i, 0)),
            pl.BlockSpec((1, d), lambda i: (0, 0)),
            pl.BlockSpec(memory_space=pl.ANY),
        ],
        out_specs=pl.BlockSpec((tt, d), lambda i: (i, 0)),
        out_shape=jax.ShapeDtypeStruct((t, d), F32),
        scratch_shapes=[pltpu.VMEM((2, TOP_K_INNER, tt, d), F32), pltpu.SemaphoreType.DMA((2,))],
        compiler_params=pltpu.CompilerParams(
            dimension_semantics=("arbitrary",), vmem_limit_bytes=VMEM_LIMIT_BYTES),
        name="moe_combine",
    )(dest3, dest3, x1, rt, gf.reshape(1, d), y)


def _dispatch_plan(experts, t):
    n_assign = t * TOP_K_INNER
    flat_e = experts.reshape(n_assign)
    onehot = (flat_e[:, None] == jnp.arange(N_EXPERTS, dtype=jnp.int32)[None, :]).astype(jnp.int32)
    csum = jnp.cumsum(onehot, axis=0)
    counts = csum[-1]
    rank = jnp.sum(onehot * csum, axis=1) - 1
    padded = (counts + DISPATCH_BLOCK - 1) // DISPATCH_BLOCK * DISPATCH_BLOCK
    pad_end = jnp.cumsum(padded)
    pad_start = pad_end - padded
    dest = jnp.sum(onehot * pad_start[None, :], axis=1) + rank
    n_rows = -(-n_assign // DISPATCH_BLOCK) * DISPATCH_BLOCK + N_EXPERTS * DISPATCH_BLOCK
    n_blocks = n_rows // DISPATCH_BLOCK
    blk_start = jnp.arange(n_blocks, dtype=jnp.int32) * DISPATCH_BLOCK
    blk_e = jnp.minimum(jnp.sum((pad_end[None, :] <= blk_start[:, None]).astype(jnp.int32), axis=1),
                        N_EXPERTS - 1).astype(jnp.int32)
    n_used = (pad_end[-1] // DISPATCH_BLOCK).astype(jnp.int32).reshape(1)
    row_tok = jnp.zeros((n_rows,), jnp.int32).at[dest].set(
        jnp.arange(n_assign, dtype=jnp.int32) // TOP_K_INNER, unique_indices=True)
    return dest.astype(jnp.int32), row_tok, blk_e, n_used


def _layer(x2, b, s, norm_mix_g, w_in, sgu_w, sgu_b, sgu_ln_g, sgu_ln_b, w_proj_attn, w_proj_sgu, w_out,
           norm_ffn_g, w_rg, b_rg, w_re, b_re, w_exp_gate, w_exp_up, w_exp_down):
    t, d = x2.shape
    proj = _inproj(x2, norm_mix_g, w_in.astype(BF16))
    attn = _attention(proj, b, s)

    n_r = N_EXPERT_GROUPS + N_EXPERTS
    w_r = jnp.concatenate([w_rg, w_re, jnp.zeros((d, LANES - n_r), F32)], axis=1)
    wr_hi = w_r.astype(BF16)
    wr_lo = (w_r - wr_hi.astype(F32)).astype(BF16)
    b_r = jnp.concatenate([b_rg, b_re, jnp.zeros((LANES - n_r,), F32)]).reshape(1, LANES)
    x1, h2, rt = _merge(x2, attn, proj, sgu_w, sgu_b, sgu_ln_g, sgu_ln_b,
                        w_proj_attn.astype(BF16), w_proj_sgu.astype(BF16), w_out.astype(BF16),
                        norm_ffn_g, wr_hi, wr_lo, b_r)

    experts = rt[:, 0:TOP_K_INNER].astype(jnp.int32)
    dest, row_tok, blk_e, n_used = _dispatch_plan(experts, t)
    tt = min(ROW_TILE, t)
    dest3 = dest.reshape(t // tt, 1, TOP_K_INNER * tt)
    y = _experts(blk_e, n_used, row_tok, h2, w_exp_gate.astype(BF16), w_exp_up.astype(BF16), w_exp_down.astype(BF16))
    return dest3, x1, rt, y


def kernel(x, norm_mix_g, w_in, sgu_w, sgu_b, sgu_ln_g, sgu_ln_b, w_proj_attn, w_proj_sgu, w_out, norm_ffn_g,
           w_router_group, b_router_group, w_router_expert, b_router_expert, w_exp_gate, w_exp_up, w_exp_down,
           norm_final_g):
    b, s, d = x.shape
    assert w_in.shape[0] == 1, "the combine kernel fuses the closing RMSNorm, so exactly one layer is supported"
    x2 = x.reshape(b * s, d)
    dest3, x1, rt, y = _layer(
        x2, b, s, norm_mix_g[0], w_in[0], sgu_w[0], sgu_b[0], sgu_ln_g[0], sgu_ln_b[0],
        w_proj_attn[0], w_proj_sgu[0], w_out[0], norm_ffn_g[0],
        w_router_group[0], b_router_group[0], w_router_expert[0], b_router_expert[0],
        w_exp_gate[0], w_exp_up[0], w_exp_down[0])
    return _combine(dest3, x1, rt, norm_final_g, y).reshape(b, s, d)
```

```python
import functools

import jax
import jax.numpy as jnp
from jax import lax
from jax.experimental import pallas as pl
from jax.experimental.pallas import tpu as pltpu

F32 = jnp.float32
BF16 = jnp.bfloat16

N_HEADS = 8
HEAD_DIM = 128
D_ATTN = N_HEADS * HEAD_DIM
MOBA_BLOCK = 256
MOBA_TOPK = 3
ROPE_THETA = 500000.0
ROPE_DIM = HEAD_DIM // 4
ROPE_HALF = ROPE_DIM // 2
N_GROUPS_SGU = 8
SGU_GROUP_DIM = 128
D_SGU = N_GROUPS_SGU * SGU_GROUP_DIM
SGU_CHUNK = 128
N_EXPERT_GROUPS = 4
EXPERTS_PER_GROUP = 8
N_EXPERTS = N_EXPERT_GROUPS * EXPERTS_PER_GROUP
TOP_K_INNER = 2
DISPATCH_BLOCK = 256
EPS = 1e-6

LANES = 128
VMEM_LIMIT_BYTES = 56 * 1024 * 1024
LOG2_E = 1.4426950408889634
MASK_BIAS = -1e30

INPROJ_TM = 1024
INPROJ_TN = 1024
MERGE_TM = 256
ROW_TILE = 256


def _dot(a, b):
    return jnp.dot(a, b, preferred_element_type=F32)


def _dot_nt(a, b):
    return lax.dot_general(a, b, (((1,), (1,)), ((), ())), preferred_element_type=F32)


def _sigmoid(x):
    return 1.0 / (1.0 + jnp.exp(-x))


def _gelu_tanh(x):
    return 0.5 * x * (1.0 + jnp.tanh(0.7978845608028654 * (x + 0.044715 * (x * x * x))))


def _inproj_kernel(x_ref, g_ref, w_ref, o_ref, h_ref):
    @pl.when(pl.program_id(1) == 0)
    def _():
        x = x_ref[...]
        ms = jnp.mean(x * x, axis=-1, keepdims=True)
        h_ref[...] = (x * lax.rsqrt(ms + EPS) * g_ref[...]).astype(BF16)

    o_ref[...] = _dot(h_ref[...], w_ref[...]).astype(o_ref.dtype)


def _inproj(x2, g, w_bf16):
    t, d = x2.shape
    n = w_bf16.shape[1]
    tm, tn = min(INPROJ_TM, t), INPROJ_TN
    return pl.pallas_call(
        _inproj_kernel,
        grid=(t // tm, n // tn),
        in_specs=[
            pl.BlockSpec((tm, d), lambda i, j: (i, 0)),
            pl.BlockSpec((1, d), lambda i, j: (0, 0)),
            pl.BlockSpec((d, tn), lambda i, j: (0, j)),
        ],
        out_specs=pl.BlockSpec((tm, tn), lambda i, j: (i, j)),
        out_shape=jax.ShapeDtypeStruct((t, n), BF16),
        scratch_shapes=[pltpu.VMEM((tm, d), BF16)],
        compiler_params=pltpu.CompilerParams(
            dimension_semantics=("parallel", "arbitrary"), vmem_limit_bytes=VMEM_LIMIT_BYTES),
        name="inproj",
    )(x2, g.reshape(1, d), w_bf16)


def _attn_kernel(q_ref, k_ref, v_ref, c_ref, s1_ref, s2_ref, o_ref, kaug_ref, km_ref, s_ref, *, n_blk):
    blk = MOBA_BLOCK
    scale = HEAD_DIM ** -0.5 * LOG2_E
    lane = lax.broadcasted_iota(jnp.int32, (blk, LANES), 1)

    def rope(xb, r0):
        c = c_ref[pl.ds(r0, blk), :]
        s1 = s1_ref[pl.ds(r0, blk), :]
        s2 = s2_ref[pl.ds(r0, blk), :]
        return (xb * c + pltpu.roll(xb, LANES - ROPE_HALF, 1) * s1
                + pltpu.roll(xb, ROPE_HALF, 1) * s2)

    km_ref[...] = jnp.zeros_like(km_ref)

    def kprep(j, carry):
        r0 = pl.multiple_of(j * blk, blk)
        kr = rope(k_ref[pl.ds(r0, blk), :].astype(F32), r0)
        kaug_ref[pl.ds(r0, blk), 0:LANES] = kr.astype(BF16)
        kaug_ref[pl.ds(r0, blk), LANES:2 * LANES] = jnp.where(lane == j, 1.0, 0.0).astype(BF16)
        km_ref[pl.ds(j, 1), :] = jnp.mean(kr, axis=0, keepdims=True)
        return carry

    lax.fori_loop(0, n_blk, kprep, 0)

    row_i = lax.broadcasted_iota(jnp.int32, (blk, blk), 0)
    col_i = lax.broadcasted_iota(jnp.int32, (blk, blk), 1)

    def make_qaug(i):
        r0 = i * blk
        qr = (rope(q_ref[r0:r0 + blk, :].astype(F32), r0) * scale).astype(BF16)
        km = km_ref[...]
        km_hi = km.astype(BF16)
        km_lo = (km - km_hi.astype(F32)).astype(BF16)
        gate = _dot_nt(qr, km_hi) + _dot_nt(qr, km_lo)
        g = jnp.where(lane < i, gate, -jnp.inf)
        sel = lane < 0
        for _ in range(MOBA_TOPK):
            mx = jnp.max(g, axis=-1, keepdims=True)
            idx = jnp.min(jnp.where(g == mx, lane, LANES), axis=-1, keepdims=True)
            hit = lane == idx
            sel = jnp.logical_or(sel, hit)
            g = jnp.where(hit, -jnp.inf, g)
        allowed = jnp.logical_or(jnp.logical_and(sel, lane < i), lane == i)
        bias = jnp.where(allowed, 0.0, MASK_BIAS).astype(BF16)
        return jnp.concatenate([qr, bias], axis=1)

    def score_chunk(i, qaug, c, fold_max):
        s = _dot_nt(qaug, kaug_ref[c * blk:(c + 1) * blk, :])
        if c == i:
            s = jnp.where(col_i <= row_i, s, -jnp.inf)
        s_ref[i % 2, :, c * blk:(c + 1) * blk] = s
        f = jnp.maximum(s[:, 0:LANES], s[:, LANES:2 * LANES])
        return f if fold_max is None else jnp.maximum(fold_max, f)

    def value_chunk(i, m, c, fold_sum, acc):
        p = jnp.exp2(s_ref[i % 2, :, c * blk:(c + 1) * blk] - m)
        f = p[:, 0:LANES] + p[:, LANES:2 * LANES]
        pv = _dot(p.astype(BF16), v_ref[c * blk:(c + 1) * blk, :])
        return (f, pv) if acc is None else (fold_sum + f, acc + pv)

    qaug = make_qaug(0)
    m_cur = jnp.max(score_chunk(0, qaug, 0, None), axis=-1, keepdims=True)
    for i in range(n_blk):
        has_next = i + 1 < n_blk
        n_a = i + 2 if has_next else 0
        n_b = i + 1
        if has_next:
            qaug = make_qaug(i + 1)
        fold_max = fold_sum = acc = None
        for c in range(max(n_a, n_b)):
            if c < n_a:
                fold_max = score_chunk(i + 1, qaug, c, fold_max)
            if c < n_b:
                fold_sum, acc = value_chunk(i, m_cur, c, fold_sum, acc)
        l = jnp.sum(fold_sum, axis=-1, keepdims=True)
        o_ref[i * blk:(i + 1) * blk, :] = (acc / l).astype(o_ref.dtype)
        if has_next:
            m_cur = jnp.max(fold_max, axis=-1, keepdims=True)


def _rope_tables(s):
    inv = ROPE_THETA ** (-jnp.arange(ROPE_HALF, dtype=F32) * 2.0 / ROPE_DIM)
    ang = jnp.arange(s).astype(F32)[:, None] * inv[None, :]
    cos, sin = jnp.cos(ang), jnp.sin(ang)
    zeros = jnp.zeros((s, LANES - ROPE_DIM), F32)
    zh = jnp.zeros((s, ROPE_HALF), F32)
    c = jnp.concatenate([cos, cos, zeros + 1.0], axis=1)
    s1 = jnp.concatenate([-sin, zh, zeros], axis=1)
    s2 = jnp.concatenate([zh, sin, zeros], axis=1)
    return c, s1, s2


def _attention(proj, b, s):
    assert s % MOBA_BLOCK == 0
    n_blk = s // MOBA_BLOCK
    assert n_blk <= LANES
    c, s1, s2 = _rope_tables(s)
    tab = pl.BlockSpec((s, LANES), lambda bi, h: (0, 0))
    return pl.pallas_call(
        functools.partial(_attn_kernel, n_blk=n_blk),
        grid=(b, N_HEADS),
        in_specs=[
            pl.BlockSpec((s, HEAD_DIM), lambda bi, h: (bi, h)),
            pl.BlockSpec((s, HEAD_DIM), lambda bi, h: (bi, N_HEADS + h)),
            pl.BlockSpec((s, HEAD_DIM), lambda bi, h: (bi, 2 * N_HEADS + h)),
            tab, tab, tab,
        ],
        out_specs=pl.BlockSpec((s, HEAD_DIM), lambda bi, h: (bi, h)),
        out_shape=jax.ShapeDtypeStruct((b * s, D_ATTN), BF16),
        scratch_shapes=[pltpu.VMEM((s, 2 * LANES), BF16), pltpu.VMEM((LANES, LANES), F32),
                        pltpu.VMEM((2, MOBA_BLOCK, s), F32)],
        compiler_params=pltpu.CompilerParams(
            dimension_semantics=("parallel", "arbitrary"), vmem_limit_bytes=VMEM_LIMIT_BYTES),
        name="moba_attn",
    )(proj, proj, proj, c, s1, s2)


def _merge_kernel(x_ref, at_ref, u_ref, vg_ref, ga0_ref, ga1_ref, gs0_ref, gs1_ref,
                  sw_ref, sb_ref, lng_ref, lnb_ref, wpa_ref, wpb_ref, wo_ref, g2_ref,
                  wrh_ref, wrl_ref, br_ref,
                  x1_ref, h2_ref, rt_ref,
                  wm_ref, vgn_ref, sgu_ref, mg_ref):
    tm = x_ref.shape[0]
    d = x_ref.shape[1]
    half = d // 2

    @pl.when(pl.program_id(0) == 0)
    def _():
        r = lax.broadcasted_iota(jnp.int32, (SGU_CHUNK, SGU_CHUNK), 0)
        c = lax.broadcasted_iota(jnp.int32, (SGU_CHUNK, SGU_CHUNK), 1)
        for g in range(N_GROUPS_SGU):
            wm_ref[g] = jnp.where(c <= r, sw_ref[g], 0.0).astype(BF16)

    vg = _gelu_tanh(vg_ref[...].astype(F32))
    mu = jnp.mean(vg, axis=-1, keepdims=True)
    var = jnp.mean(jnp.square(vg - mu), axis=-1, keepdims=True)
    vgn_ref[...] = ((vg - mu) * lax.rsqrt(var + EPS) * lng_ref[...] + lnb_ref[...]).astype(BF16)
    for c in range(tm // SGU_CHUNK):
        rows = slice(c * SGU_CHUNK, (c + 1) * SGU_CHUNK)
        for g in range(N_GROUPS_SGU):
            cols = slice(g * SGU_GROUP_DIM, (g + 1) * SGU_GROUP_DIM)
            mixed = _dot(wm_ref[g], vgn_ref[rows, cols]) + sb_ref[:, cols]
            u = _gelu_tanh(u_ref[rows, cols].astype(F32))
            sgu_ref[rows, cols] = (u * mixed).astype(BF16)

    a = _dot(at_ref[...], wpa_ref[...])
    s = _dot(sgu_ref[...], wpb_ref[...])
    mg_ref[:, 0:half] = (_sigmoid(ga0_ref[...].astype(F32)) * a[:, 0:half]
                         + _sigmoid(gs0_ref[...].astype(F32)) * s[:, 0:half]).astype(BF16)
    mg_ref[:, half:d] = (_sigmoid(ga1_ref[...].astype(F32)) * a[:, half:d]
                         + _sigmoid(gs1_ref[...].astype(F32)) * s[:, half:d]).astype(BF16)
    x1 = x_ref[...] + _dot(mg_ref[...], wo_ref[...])
    x1_ref[...] = x1

    h2 = x1 * lax.rsqrt(jnp.mean(x1 * x1, axis=-1, keepdims=True) + EPS) * g2_ref[...]
    h2_ref[...] = h2
    hi = h2.astype(BF16)
    lo = (h2 - hi.astype(F32)).astype(BF16)
    logits = (_dot(hi, wrh_ref[...]) + _dot(lo, wrh_ref[...]) + _dot(hi, wrl_ref[...])) + br_ref[...]
    lane = lax.broadcasted_iota(jnp.int32, (tm, LANES), 1)
    neg = -jnp.inf
    gl = jnp.where(lane < N_EXPERT_GROUPS, logits, neg)
    gmax = jnp.max(gl, axis=-1, keepdims=True)
    grp = jnp.min(jnp.where(gl == gmax, lane, LANES), axis=-1, keepdims=True)
    p_grp = 1.0 / jnp.sum(jnp.exp(gl - gmax), axis=-1, keepdims=True)
    e0 = N_EXPERT_GROUPS + grp * EXPERTS_PER_GROUP
    emask = jnp.logical_and(lane >= e0, lane < e0 + EXPERTS_PER_GROUP)
    el = jnp.where(emask, logits, neg)
    emax = jnp.max(el, axis=-1, keepdims=True)
    ex = jnp.exp(el - emax)
    probs = jnp.where(emask, ex / jnp.sum(ex, axis=-1, keepdims=True), -1.0)
    p1 = jnp.max(probs, axis=-1, keepdims=True)
    i1 = jnp.min(jnp.where(probs == p1, lane, LANES), axis=-1, keepdims=True)
    probs2 = jnp.where(lane == i1, -1.0, probs)
    p2 = jnp.max(probs2, axis=-1, keepdims=True)
    i2 = jnp.min(jnp.where(probs2 == p2, lane, LANES), axis=-1, keepdims=True)
    den = p1 + p2
    w1 = p_grp * (p1 / den)
    w2 = p_grp * (p2 / den)
    ex1 = (i1 - N_EXPERT_GROUPS).astype(F32)
    ex2 = (i2 - N_EXPERT_GROUPS).astype(F32)
    rt_ref[...] = jnp.where(lane == 0, ex1, jnp.where(lane == 1, ex2,
                            jnp.where(lane == 2, w1, jnp.where(lane == 3, w2, 0.0))))


def _merge(x2, attn, proj, sgu_w, sgu_b, ln_g, ln_b, wpa, wpb, wo, g2, wr_hi, wr_lo, br):
    t, d = x2.shape
    tm = MERGE_TM
    assert d == 2 * D_ATTN and t % tm == 0
    u_blk = 3 * D_ATTN // D_SGU
    sb_full = jnp.repeat(sgu_b.T, SGU_GROUP_DIM, axis=1)

    def col(j):
        return pl.BlockSpec((tm, D_SGU), lambda i: (i, j))

    def const(shape):
        return pl.BlockSpec(shape, lambda i: tuple(0 for _ in shape))

    return pl.pallas_call(
        _merge_kernel,
        grid=(t // tm,),
        in_specs=[
            pl.BlockSpec((tm, d), lambda i: (i, 0)),
            pl.BlockSpec((tm, D_ATTN), lambda i: (i, 0)),
            col(u_blk), col(u_blk + 1), col(u_blk + 2), col(u_blk + 3), col(u_blk + 4), col(u_blk + 5),
            const((N_GROUPS_SGU, SGU_CHUNK, SGU_CHUNK)),
            const((SGU_CHUNK, D_SGU)),
            const((1, D_SGU)), const((1, D_SGU)),
            const((D_ATTN, d)), const((D_SGU, d)), const((d, d)),
            const((1, d)),
            const((d, LANES)), const((d, LANES)), const((1, LANES)),
        ],
        out_specs=[
            pl.BlockSpec((tm, d), lambda i: (i, 0)),
            pl.BlockSpec((tm, d), lambda i: (i, 0)),
            pl.BlockSpec((tm, LANES), lambda i: (i, 0)),
        ],
        out_shape=[
            jax.ShapeDtypeStruct((t, d), F32),
            jax.ShapeDtypeStruct((t, d), F32),
            jax.ShapeDtypeStruct((t, LANES), F32),
        ],
        scratch_shapes=[
            pltpu.VMEM((N_GROUPS_SGU, SGU_CHUNK, SGU_CHUNK), BF16),
            pltpu.VMEM((tm, D_SGU), BF16),
            pltpu.VMEM((tm, D_SGU), BF16),
            pltpu.VMEM((tm, d), BF16),
        ],
        compiler_params=pltpu.CompilerParams(
            dimension_semantics=("arbitrary",), vmem_limit_bytes=VMEM_LIMIT_BYTES),
        name="merge_router",
    )(x2, attn, proj, proj, proj, proj, proj, proj, sgu_w, sb_full,
      ln_g.reshape(1, -1), ln_b.reshape(1, -1), wpa, wpb, wo, g2.reshape(1, -1), wr_hi, wr_lo, br)


def _row_copy(src, si, dst, di, sem):
    return pltpu.make_async_copy(src.at[pl.ds(si, 1)], dst.at[pl.ds(di, 1)], sem)


def _gather_block_rows(tok_ref, src_hbm, dst, sem):
    for r in range(DISPATCH_BLOCK):
        _row_copy(src_hbm, tok_ref[0, 0, r], dst, r, sem).start()


def _wait_block_rows(src_hbm, dst, sem):
    pltpu.make_async_copy(src_hbm.at[pl.ds(0, DISPATCH_BLOCK)], dst, sem).wait()


def _expert_kernel(be_ref, nu_ref, tok_ref, tok_next_ref, h2_hbm, wg_ref, wu_ref, wd_ref, y_ref, xbuf, sems):
    del be_ref
    i = pl.program_id(0)
    n_used = nu_ref[0]
    slot = i % 2

    @pl.when(jnp.logical_and(i == 0, n_used > 0))
    def _():
        _gather_block_rows(tok_ref, h2_hbm, xbuf.at[0], sems.at[0])

    @pl.when(i + 1 < n_used)
    def _():
        _gather_block_rows(tok_next_ref, h2_hbm, xbuf.at[1 - slot], sems.at[1 - slot])

    @pl.when(i < n_used)
    def _():
        _wait_block_rows(h2_hbm, xbuf.at[slot], sems.at[slot])
        h = xbuf[slot].astype(BF16)
        g = _dot(h, wg_ref[0])
        u = _dot(h, wu_ref[0])
        hid = (g * _sigmoid(g) * u).astype(BF16)
        y_ref[...] = _dot(hid, wd_ref[0])

    @pl.when(i >= n_used)
    def _():
        y_ref[...] = jnp.zeros_like(y_ref)


def _experts(blk_e, n_used, row_tok, h2, wg, wu, wd):
    d = h2.shape[1]
    ff = wg.shape[2]
    n_blocks = row_tok.shape[0] // DISPATCH_BLOCK
    tok3 = row_tok.reshape(n_blocks, 1, DISPATCH_BLOCK)
    grid_spec = pltpu.PrefetchScalarGridSpec(
        num_scalar_prefetch=2,
        grid=(n_blocks,),
        in_specs=[
            pl.BlockSpec((1, 1, DISPATCH_BLOCK), lambda i, be, nu: (i, 0, 0), memory_space=pltpu.SMEM),
            pl.BlockSpec((1, 1, DISPATCH_BLOCK), lambda i, be, nu: (jnp.minimum(i + 1, n_blocks - 1), 0, 0),
                         memory_space=pltpu.SMEM),
            pl.BlockSpec(memory_space=pl.ANY),
            pl.BlockSpec((1, d, ff), lambda i, be, nu: (be[i], 0, 0)),
            pl.BlockSpec((1, d, ff), lambda i, be, nu: (be[i], 0, 0)),
            pl.BlockSpec((1, ff, d), lambda i, be, nu: (be[i], 0, 0)),
        ],
        out_specs=pl.BlockSpec((DISPATCH_BLOCK, d), lambda i, be, nu: (i, 0)),
        scratch_shapes=[pltpu.VMEM((2, DISPATCH_BLOCK, d), F32), pltpu.SemaphoreType.DMA((2,))],
    )
    return pl.pallas_call(
        _expert_kernel,
        grid_spec=grid_spec,
        out_shape=jax.ShapeDtypeStruct((n_blocks * DISPATCH_BLOCK, d), F32),
        compiler_params=pltpu.CompilerParams(
            dimension_semantics=("arbitrary",), vmem_limit_bytes=VMEM_LIMIT_BYTES),
        name="moe_experts",
    )(blk_e, n_used, tok3, tok3, h2, wg, wu, wd)


def _combine_kernel(dest_ref, dest_next_ref, x1_ref, rt_ref, gf_ref, y_hbm, o_ref, ybuf, sems):
    tt = x1_ref.shape[0]
    i = pl.program_id(0)
    slot = i % 2

    def gather(d_ref, s):
        for t in range(tt):
            for k in range(TOP_K_INNER):
                _row_copy(y_hbm, d_ref[0, 0, TOP_K_INNER * t + k], ybuf.at[s].at[k], t, sems.at[s]).start()

    @pl.when(i == 0)
    def _():
        gather(dest_ref, 0)

    @pl.when(i + 1 < pl.num_programs(0))
    def _():
        gather(dest_next_ref, 1 - slot)

    for k in range(TOP_K_INNER):
        pltpu.make_async_copy(y_hbm.at[pl.ds(0, tt)], ybuf.at[slot].at[k], sems.at[slot]).wait()

    rt = rt_ref[...]
    w1 = rt[:, 2:3]
    w2 = rt[:, 3:4]
    z = x1_ref[...] + (w1 * ybuf[slot, 0] + w2 * ybuf[slot, 1])
    o_ref[...] = z * lax.rsqrt(jnp.mean(z * z, axis=-1, keepdims=True) + EPS) * gf_ref[...]


def _combine(dest3, x1, rt, gf, y):
    t, d = x1.shape
    tt = dest3.shape[2] // TOP_K_INNER
    n_tiles = t // tt
    return pl.pallas_call(
        _combine_kernel,
        grid=(n_tiles,),
        in_specs=[
            pl.BlockSpec((1, 1, TOP_K_INNER * tt), lambda i: (i, 0, 0), memory_space=pltpu.SMEM),
            pl.BlockSpec((1, 1, TOP_K_INNER * tt), lambda i: (jnp.minimum(i + 1, n_tiles - 1), 0, 0),
                         memory_space=pltpu.SMEM),
            pl.BlockSpec((tt, d), lambda i: (i, 0)),
            pl.BlockSpec((tt, LANES), lambda i: (i, 0)),
            pl.BlockSpec((1, d), lambda i: (0, 0)),
            pl.BlockSpec(memory_space=pl.ANY),
        ],
        out_specs=pl.BlockSpec((tt, d), lambda i: (i, 0)),
        out_shape=jax.ShapeDtypeStruct((t, d), F32),
        scratch_shapes=[pltpu.VMEM((2, TOP_K_INNER, tt, d), F32), pltpu.SemaphoreType.DMA((2,))],
        compiler_params=pltpu.CompilerParams(
            dimension_semantics=("arbitrary",), vmem_limit_bytes=VMEM_LIMIT_BYTES),
        name="moe_combine",
    )(dest3, dest3, x1, rt, gf.reshape(1, d), y)


def _dispatch_plan(experts, t):
    n_assign = t * TOP_K_INNER
    flat_e = experts.reshape(n_assign)
    onehot = (flat_e[:, None] == jnp.arange(N_EXPERTS, dtype=jnp.int32)[None, :]).astype(jnp.int32)
    csum = jnp.cumsum(onehot, axis=0)
    counts = csum[-1]
    rank = jnp.sum(onehot * csum, axis=1) - 1
    padded = (counts + DISPATCH_BLOCK - 1) // DISPATCH_BLOCK * DISPATCH_BLOCK
    pad_end = jnp.cumsum(padded)
    pad_start = pad_end - padded
    dest = jnp.sum(onehot * pad_start[None, :], axis=1) + rank
    n_rows = -(-n_assign // DISPATCH_BLOCK) * DISPATCH_BLOCK + N_EXPERTS * DISPATCH_BLOCK
    n_blocks = n_rows // DISPATCH_BLOCK
    blk_start = jnp.arange(n_blocks, dtype=jnp.int32) * DISPATCH_BLOCK
    blk_e = jnp.minimum(jnp.sum((pad_end[None, :] <= blk_start[:, None]).astype(jnp.int32), axis=1),
                        N_EXPERTS - 1).astype(jnp.int32)
    n_used = (pad_end[-1] // DISPATCH_BLOCK).astype(jnp.int32).reshape(1)
    row_tok = jnp.zeros((n_rows,), jnp.int32).at[dest].set(
        jnp.arange(n_assign, dtype=jnp.int32) // TOP_K_INNER, unique_indices=True)
    return dest.astype(jnp.int32), row_tok, blk_e, n_used


def _layer(x2, b, s, norm_mix_g, w_in, sgu_w, sgu_b, sgu_ln_g, sgu_ln_b, w_proj_attn, w_proj_sgu, w_out,
           norm_ffn_g, w_rg, b_rg, w_re, b_re, w_exp_gate, w_exp_up, w_exp_down):
    t, d = x2.shape
    proj = _inproj(x2, norm_mix_g, w_in.astype(BF16))
    attn = _attention(proj, b, s)

    n_r = N_EXPERT_GROUPS + N_EXPERTS
    w_r = jnp.concatenate([w_rg, w_re, jnp.zeros((d, LANES - n_r), F32)], axis=1)
    wr_hi = w_r.astype(BF16)
    wr_lo = (w_r - wr_hi.astype(F32)).astype(BF16)
    b_r = jnp.concatenate([b_rg, b_re, jnp.zeros((LANES - n_r,), F32)]).reshape(1, LANES)
    x1, h2, rt = _merge(x2, attn, proj, sgu_w, sgu_b, sgu_ln_g, sgu_ln_b,
                        w_proj_attn.astype(BF16), w_proj_sgu.astype(BF16), w_out.astype(BF16),
                        norm_ffn_g, wr_hi, wr_lo, b_r)

    experts = rt[:, 0:TOP_K_INNER].astype(jnp.int32)
    dest, row_tok, blk_e, n_used = _dispatch_plan(experts, t)
    tt = min(ROW_TILE, t)
    dest3 = dest.reshape(t // tt, 1, TOP_K_INNER * tt)
    y = _experts(blk_e, n_used, row_tok, h2, w_exp_gate.astype(BF16), w_exp_up.astype(BF16), w_exp_down.astype(BF16))
    return dest3, x1, rt, y


def kernel(x, norm_mix_g, w_in, sgu_w, sgu_b, sgu_ln_g, sgu_ln_b, w_proj_attn, w_proj_sgu, w_out, norm_ffn_g,
           w_router_group, b_router_group, w_router_expert, b_router_expert, w_exp_gate, w_exp_up, w_exp_down,
           norm_final_g):
    b, s, d = x.shape
    assert w_in.shape[0] == 1, "the combine kernel fuses the closing RMSNorm, so exactly one layer is supported"
    x2 = x.reshape(b * s, d)
    dest3, x1, rt, y = _layer(
        x2, b, s, norm_mix_g[0], w_in[0], sgu_w[0], sgu_b[0], sgu_ln_g[0], sgu_ln_b[0],
        w_proj_attn[0], w_proj_sgu[0], w_out[0], norm_ffn_g[0],
        w_router_group[0], b_router_group[0], w_router_expert[0], b_router_expert[0],
        w_exp_gate[0], w_exp_up[0], w_exp_down[0])
    return _combine(dest3, x1, rt, norm_final_g, y).reshape(b, s, d)
```
